```python
import math
import jax, jax.numpy as jnp
from jax import lax
import numpy as np

D_MODEL = 1024
BATCH = 8
SEQ = 4096
DEPTH = 4

MEM_LEN = 256
EPS = 1e-6
NEG_INF = -1e30

CONV_DIM = D_MODEL
CONV_KERNEL = 31

SSD_INNER = 2 * D_MODEL
SSD_HEAD_DIM = 64
SSD_HEADS = SSD_INNER // SSD_HEAD_DIM
SSD_GROUPS = 4
SSD_STATE = 128
SSD_CONV = 4
SSD_CHUNK = 128
SSD_XBC = SSD_INNER + 2 * SSD_GROUPS * SSD_STATE

ATTN_HEADS = 16
ATTN_KV_HEADS = 4
ATTN_HEAD_DIM = 64
ATTN_DIM = ATTN_HEADS * ATTN_HEAD_DIM
ATTN_KV_DIM = ATTN_KV_HEADS * ATTN_HEAD_DIM
ATTN_WINDOW = 128
ATTN_BLOCK = 128

REL_BUCKETS = 32
REL_MAX_DIST = 128

XATTN_HEADS = 4
XATTN_HEAD_DIM = D_MODEL // XATTN_HEADS

N_BRANCH = 3
MLP_HIDDEN = 4 * D_MODEL

OFF_CONV = 0
OFF_Z = OFF_CONV + 2 * CONV_DIM
OFF_XBC = OFF_Z + SSD_INNER
OFF_DT = OFF_XBC + SSD_XBC
OFF_Q = OFF_DT + SSD_HEADS
OFF_K = OFF_Q + ATTN_DIM
OFF_V = OFF_K + ATTN_KV_DIM
OFF_GATE = OFF_V + ATTN_KV_DIM
IN_COLS = OFF_GATE + N_BRANCH * D_MODEL

kernel_name = "hybrid_conv_ssd_swa_gated_trunk"


def rms_norm(x, g):
    x32 = x.astype(jnp.float32)
    y = x32 * lax.rsqrt(jnp.mean(x32 * x32, axis=-1, keepdims=True) + EPS)
    return (y * g.astype(jnp.float32)).astype(x.dtype)


def layer_norm(x, g, b):
    x32 = x.astype(jnp.float32)
    mu = jnp.mean(x32, axis=-1, keepdims=True)
    xc = x32 - mu
    y = xc * lax.rsqrt(jnp.mean(xc * xc, axis=-1, keepdims=True) + EPS)
    return (y * g.astype(jnp.float32) + b.astype(jnp.float32)).astype(x.dtype)


def causal_depthwise_conv(x, w, b):
    k, c = w.shape
    y = lax.conv_general_dilated(
        x, w[:, None, :].astype(x.dtype), window_strides=(1,), padding=[(k - 1, 0)],
        dimension_numbers=("NWC", "WIO", "NWC"), feature_group_count=c)
    return y + b.astype(x.dtype)


def conformer_conv_branch(u, dw_w, dw_b, ln_g, ln_b):
    a, gate = jnp.split(u, 2, axis=-1)
    h = a * jax.nn.sigmoid(gate)
    h = causal_depthwise_conv(h, dw_w, dw_b)
    h = layer_norm(h, ln_g, ln_b)
    return jax.nn.silu(h)


def segsum_exp(a):
    t = a.shape[-1]
    cs = jnp.cumsum(a, axis=-1)
    diff = cs[..., :, None] - cs[..., None, :]
    mask = jnp.tril(jnp.ones((t, t), dtype=bool))
    return jnp.where(mask, jnp.exp(jnp.where(mask, diff, 0.0)), 0.0)


def ssd_chunked(x, dt, a, bm, cm):
    b, l, h, p = x.shape
    g, n = bm.shape[-2:]
    r = h // g
    q = SSD_CHUNK
    nc = l // q
    xdt = (x * dt[..., None].astype(x.dtype)).reshape(b, nc, q, g, r, p)
    bc = bm.reshape(b, nc, q, g, n)
    cc = cm.reshape(b, nc, q, g, n)
    da = (dt * a).reshape(b, nc, q, g, r).transpose(0, 1, 3, 4, 2)
    cs = jnp.cumsum(da, axis=-1)
    decay = segsum_exp(da).astype(x.dtype)
    cb = jnp.einsum("bcqgn,bcsgn->bcgqs", cc, bc)
    scores = cb[:, :, :, None] * decay
    y_diag = jnp.einsum("bcgrqs,bcsgrp->bcqgrp", scores, xdt)
    decay_to_end = jnp.exp(cs[..., -1:] - cs).astype(x.dtype)
    chunk_states = jnp.einsum("bcqgn,bcgrq,bcqgrp->bcgrpn", bc, decay_to_end, xdt)
    chunk_decay = jnp.exp(cs[..., -1])

    def step(state, inp):
        s_c, d_c = inp
        return state * d_c[..., None, None] + s_c, state

    init = jnp.zeros((b, g, r, p, n), jnp.float32)
    _, states_in = lax.scan(
        step, init,
        (jnp.moveaxis(chunk_states.astype(jnp.float32), 1, 0), jnp.moveaxis(chunk_decay, 1, 0)))
    states_in = jnp.moveaxis(states_in, 0, 1).astype(x.dtype)
    decay_from_start = jnp.exp(cs).astype(x.dtype)
    y_off = jnp.einsum("bcqgn,bcgrpn,bcgrq->bcqgrp", cc, states_in, decay_from_start)
    return (y_diag + y_off).reshape(b, l, h, p)


def ssd_branch(z, xbc, dt_raw, conv_w, conv_b, dt_bias, a_log, d_skip, norm_g):
    b, l, _ = xbc.shape
    xbc = jax.nn.silu(causal_depthwise_conv(xbc, conv_w, conv_b))
    xs = xbc[..., :SSD_INNER].reshape(b, l, SSD_HEADS, SSD_HEAD_DIM)
    bm = xbc[..., SSD_INNER:SSD_INNER + SSD_GROUPS * SSD_STATE].reshape(b, l, SSD_GROUPS, SSD_STATE)
    cm = xbc[..., SSD_INNER + SSD_GROUPS * SSD_STATE:].reshape(b, l, SSD_GROUPS, SSD_STATE)
    dt = jax.nn.softplus(dt_raw.astype(jnp.float32) + dt_bias.astype(jnp.float32))
    a = -jnp.exp(a_log.astype(jnp.float32))
    y = ssd_chunked(xs, dt, a, bm, cm) + xs * d_skip[:, None].astype(xs.dtype)
    y = y.reshape(b, l, SSD_INNER) * jax.nn.silu(z)
    y = rms_norm(y.reshape(b, l, SSD_GROUPS, SSD_INNER // SSD_GROUPS),
                 norm_g.reshape(SSD_GROUPS, SSD_INNER // SSD_GROUPS))
    return y.reshape(b, l, SSD_INNER)


def t5_band_bias(rel_table):
    qi = jnp.arange(ATTN_BLOCK)[:, None] + ATTN_BLOCK
    kj = jnp.arange(2 * ATTN_BLOCK)[None, :]
    dist = qi - kj
    max_exact = REL_BUCKETS // 2
    d = jnp.maximum(dist, 1).astype(jnp.float32)
    large = max_exact + (jnp.log(d / max_exact) / math.log(REL_MAX_DIST / max_exact)
                         * (REL_BUCKETS - max_exact)).astype(jnp.int32)
    large = jnp.minimum(large, REL_BUCKETS - 1)
    bucket = jnp.where(dist < max_exact, jnp.maximum(dist, 0), large)
    bias = jnp.transpose(rel_table[bucket], (2, 0, 1)).astype(jnp.float32)
    return bias, dist


def swa_branch(q, k, v, q_g, k_g, sinks, rel_bias, band_dist):
    b, l, _ = q.shape
    nb = l // ATTN_BLOCK
    r = ATTN_HEADS // ATTN_KV_HEADS
    q = rms_norm(q.reshape(b, l, ATTN_HEADS, ATTN_HEAD_DIM), q_g)
    k = rms_norm(k.reshape(b, l, ATTN_KV_HEADS, ATTN_HEAD_DIM), k_g)
    v = v.reshape(b, l, ATTN_KV_HEADS, ATTN_HEAD_DIM)

    def band(t):
        tp = jnp.pad(t, ((0, 0), (ATTN_BLOCK, 0), (0, 0), (0, 0)))
        prev = tp[:, :l].reshape(b, nb, ATTN_BLOCK, ATTN_KV_HEADS, ATTN_HEAD_DIM)
        cur = t.reshape(b, nb, ATTN_BLOCK, ATTN_KV_HEADS, ATTN_HEAD_DIM)
        return jnp.concatenate([prev, cur], axis=2)

    kb, vb = band(k), band(v)
    qb = q.reshape(b, nb, ATTN_BLOCK, ATTN_KV_HEADS, r, ATTN_HEAD_DIM)
    logits = jnp.einsum("bnqgrd,bnkgd->bngrqk", qb, kb).astype(jnp.float32) * (ATTN_HEAD_DIM ** -0.5)
    logits = logits + rel_bias.reshape(ATTN_KV_HEADS, r, ATTN_BLOCK, 2 * ATTN_BLOCK)
    key_pos = (jnp.arange(nb)[:, None] * ATTN_BLOCK - ATTN_BLOCK
               + jnp.arange(2 * ATTN_BLOCK)[None, :])
    in_window = (band_dist >= 0) & (band_dist < ATTN_WINDOW)
    mask = in_window[None] & (key_pos >= 0)[:, None, :]
    logits = jnp.where(mask[None, :, None, None], logits, NEG_INF)
    sink = sinks.astype(jnp.float32).reshape(ATTN_KV_HEADS, r)[None, None, :, :, None, None]
    m = jnp.maximum(jnp.max(logits, axis=-1, keepdims=True), sink)
    pexp = jnp.exp(logits - m)
    probs = pexp / (jnp.sum(pexp, axis=-1, keepdims=True) + jnp.exp(sink - m))
    out = jnp.einsum("bngrqk,bnkgd->bnqgrd", probs.astype(v.dtype), vb)
    return out.reshape(b, l, ATTN_DIM)


def memory_cross_attention(h, mem_h, w_q, w_kv, q_g, k_g, w_o):
    b, l, _ = h.shape
    m = mem_h.shape[1]
    q = rms_norm((h @ w_q).reshape(b, l, XATTN_HEADS, XATTN_HEAD_DIM), q_g)
    kv = mem_h @ w_kv
    k = rms_norm(kv[..., :D_MODEL].reshape(b, m, XATTN_HEADS, XATTN_HEAD_DIM), k_g)
    v = kv[..., D_MODEL:].reshape(b, m, XATTN_HEADS, XATTN_HEAD_DIM)
    logits = jnp.einsum("bqhd,bkhd->bhqk", q, k).astype(jnp.float32) * (XATTN_HEAD_DIM ** -0.5)
    probs = jax.nn.softmax(logits, axis=-1)
    out = jnp.einsum("bhqk,bkhd->bqhd", probs.astype(v.dtype), v).reshape(b, l, D_MODEL)
    return out @ w_o


def setup_inputs(seed: int = 0) -> dict:
    key = jax.random.key(seed)
    ks = iter(jax.random.split(key, 48))
    f32 = jnp.float32
    L = DEPTH

    def nrm(shape, scale):
        return jax.random.normal(next(ks), shape, f32) * scale

    def gain(shape):
        return 1.0 + nrm(shape, 0.02)

    out_scale = (2.0 * DEPTH) ** -0.5
    dt0 = jnp.exp(jax.random.uniform(next(ks), (L, SSD_HEADS), f32, math.log(1e-3), math.log(1e-1)))
    return {
        "x": nrm((BATCH, SEQ, D_MODEL), 1.0),
        "mem": nrm((BATCH, MEM_LEN, D_MODEL), 1.0),
        "rel_table": nrm((REL_BUCKETS, ATTN_HEADS), 0.1),
        "norm_mix": gain((L, D_MODEL)),
        "w_in": nrm((L, D_MODEL, IN_COLS), D_MODEL ** -0.5),
        "gate_bias": nrm((L, N_BRANCH, D_MODEL), 0.01),
        "conv_dw_w": nrm((L, CONV_KERNEL, CONV_DIM), CONV_KERNEL ** -0.5),
        "conv_dw_b": nrm((L, CONV_DIM), 0.01),
        "conv_ln_g": gain((L, CONV_DIM)),
        "conv_ln_b": nrm((L, CONV_DIM), 0.01),
        "w_conv_out": nrm((L, CONV_DIM, D_MODEL), CONV_DIM ** -0.5),
        "ssd_conv_w": nrm((L, SSD_CONV, SSD_XBC), SSD_CONV ** -0.5),
        "ssd_conv_b": nrm((L, SSD_XBC), 0.01),
        "ssd_dt_bias": dt0 + jnp.log(-jnp.expm1(-dt0)),
        "ssd_A_log": jnp.log(jax.random.uniform(next(ks), (L, SSD_HEADS), f32, 1.0, 16.0)),
        "ssd_D": gain((L, SSD_HEADS)),
        "ssd_norm_g": gain((L, SSD_INNER)),
        "w_ssd_out": nrm((L, SSD_INNER, D_MODEL), SSD_INNER ** -0.5),
        "attn_q_norm": gain((L, ATTN_HEAD_DIM)),
        "attn_k_norm": gain((L, ATTN_HEAD_DIM)),
        "attn_sinks": nrm((L, ATTN_HEADS), 0.5),
        "w_attn_out": nrm((L, ATTN_DIM, D_MODEL), ATTN_DIM ** -0.5),
        "w_mix_out": nrm((L, D_MODEL, D_MODEL), out_scale * D_MODEL ** -0.5),
        "norm_xattn": gain((L, D_MODEL)),
        "norm_mem": gain((L, D_MODEL)),
        "w_xq": nrm((L, D_MODEL, D_MODEL), D_MODEL ** -0.5),
        "w_xkv": nrm((L, D_MODEL, 2 * D_MODEL), D_MODEL ** -0.5),
        "xattn_q_norm": gain((L, XATTN_HEAD_DIM)),
        "xattn_k_norm": gain((L, XATTN_HEAD_DIM)),
        "w_xo": nrm((L, D_MODEL, D_MODEL), out_scale * D_MODEL ** -0.5),
        "norm_mlp": gain((L, D_MODEL)),
        "w_mlp_up": nrm((L, D_MODEL, MLP_HIDDEN), D_MODEL ** -0.5),
        "w_mlp_down": nrm((L, MLP_HIDDEN, D_MODEL), out_scale * MLP_HIDDEN ** -0.5),
    }


def reference(x, mem, rel_table, norm_mix, w_in, gate_bias, conv_dw_w, conv_dw_b, conv_ln_g,
              conv_ln_b, w_conv_out, ssd_conv_w, ssd_conv_b, ssd_dt_bias, ssd_A_log, ssd_D,
              ssd_norm_g, w_ssd_out, attn_q_norm, attn_k_norm, attn_sinks, w_attn_out, w_mix_out,
              norm_xattn, norm_mem, w_xq, w_xkv, xattn_q_norm, xattn_k_norm, w_xo, norm_mlp,
              w_mlp_up, w_mlp_down):
    b, l, _ = x.shape
    rel_bias, band_dist = t5_band_bias(rel_table)
    h = x
    for i in range(DEPTH):
        u = rms_norm(h, norm_mix[i])
        proj = u @ w_in[i]
        y_a = conformer_conv_branch(proj[..., OFF_CONV:OFF_Z], conv_dw_w[i], conv_dw_b[i],
                                    conv_ln_g[i], conv_ln_b[i]) @ w_conv_out[i]
        y_b = ssd_branch(proj[..., OFF_Z:OFF_XBC], proj[..., OFF_XBC:OFF_DT], proj[..., OFF_DT:OFF_Q],
                         ssd_conv_w[i], ssd_conv_b[i], ssd_dt_bias[i], ssd_A_log[i], ssd_D[i],
                         ssd_norm_g[i]) @ w_ssd_out[i]
        y_c = swa_branch(proj[..., OFF_Q:OFF_K], proj[..., OFF_K:OFF_V], proj[..., OFF_V:OFF_GATE],
                         attn_q_norm[i], attn_k_norm[i], attn_sinks[i], rel_bias, band_dist) @ w_attn_out[i]
        gates = jax.nn.sigmoid(proj[..., OFF_GATE:IN_COLS].reshape(b, l, N_BRANCH, D_MODEL)
                               + gate_bias[i].astype(proj.dtype))
        merged = gates[..., 0, :] * y_a + gates[..., 1, :] * y_b + gates[..., 2, :] * y_c
        h = h + merged @ w_mix_out[i]
        h = h + memory_cross_attention(rms_norm(h, norm_xattn[i]), rms_norm(mem, norm_mem[i]),
                                       w_xq[i], w_xkv[i], xattn_q_norm[i], xattn_k_norm[i], w_xo[i])
        u = rms_norm(h, norm_mlp[i])
        h = h + jnp.square(jax.nn.relu(u @ w_mlp_up[i])) @ w_mlp_down[i]
    return h
```

```python
import functools
import math

import numpy as np
import jax
import jax.numpy as jnp
from jax import lax
from jax.experimental import pallas as pl
from jax.experimental.pallas import tpu as pltpu

F32 = jnp.float32
BF16 = jnp.bfloat16

D_MODEL = 1024
MEM_LEN = 256
EPS = 1e-6
NEG_INF = -1e30

CONV_DIM = D_MODEL
CONV_KERNEL = 31

SSD_INNER = 2 * D_MODEL
SSD_HEAD_DIM = 64
SSD_HEADS = SSD_INNER // SSD_HEAD_DIM
SSD_GROUPS = 4
SSD_STATE = 128
SSD_CONV = 4
SSD_CHUNK = 128
SSD_BC = SSD_GROUPS * SSD_STATE
SSD_XBC = SSD_INNER + 2 * SSD_BC
SSD_GROUP_W = SSD_INNER // SSD_GROUPS

ATTN_HEADS = 16
ATTN_KV_HEADS = 4
ATTN_HEAD_DIM = 64
ATTN_DIM = ATTN_HEADS * ATTN_HEAD_DIM
ATTN_KV_DIM = ATTN_KV_HEADS * ATTN_HEAD_DIM
ATTN_REP = ATTN_HEADS // ATTN_KV_HEADS
ATTN_WINDOW = 128
ATTN_BLOCK = 128
ATTN_GROUP_W = ATTN_REP * ATTN_HEAD_DIM

REL_BUCKETS = 32
REL_MAX_DIST = 128

XATTN_HEADS = 4
XATTN_HEAD_DIM = D_MODEL // XATTN_HEADS

N_BRANCH = 3
MLP_HIDDEN = 4 * D_MODEL

OFF_CONV = 0
OFF_Z = OFF_CONV + 2 * CONV_DIM
OFF_XBC = OFF_Z + SSD_INNER
OFF_DT = OFF_XBC + SSD_XBC
OFF_Q = OFF_DT + SSD_HEADS
OFF_K = OFF_Q + ATTN_DIM
OFF_V = OFF_K + ATTN_KV_DIM
OFF_GATE = OFF_V + ATTN_KV_DIM
IN_COLS = OFF_GATE + N_BRANCH * D_MODEL

LANES = 128
SUBLANES = 8

P_GATE = 0
P_XBC = P_GATE + N_BRANCH * D_MODEL
P_CONV = P_XBC + SSD_XBC
P_Z = P_CONV + 2 * CONV_DIM
P_Q = P_Z + SSD_INNER
P_K = P_Q + ATTN_DIM
P_V = P_K + ATTN_KV_DIM
P_COLS = P_V + ATTN_KV_DIM
DT_PAD = LANES

VMEM_LIMIT = 56 * 1024 * 1024


def _cparams(semantics):
    return pltpu.CompilerParams(dimension_semantics=semantics, vmem_limit_bytes=VMEM_LIMIT)


def _const_spec(shape):
    nd = len(shape)
    return pl.BlockSpec(shape, lambda *_: (0,) * nd, pipeline_mode=pl.Buffered(1))


def _sigmoid(x):
    return 1.0 / (1.0 + jnp.exp(-x))


def _silu(x):
    return x * _sigmoid(x)


def _rms(x, g):
    return x * lax.rsqrt(jnp.mean(x * x, axis=-1, keepdims=True) + EPS) * g


def _dot(a, b):
    return jnp.dot(a, b, preferred_element_type=F32)


def _dot_nt(a, b):
    return lax.dot_general(a, b, (((1,), (1,)), ((), ())), preferred_element_type=F32)


def _split3(x):
    hi = x.astype(BF16)
    r1 = x - hi.astype(F32)
    mid = r1.astype(BF16)
    lo = (r1 - mid.astype(F32)).astype(BF16)
    return hi, mid, lo


def _inproj_kernel(x_ref, g_ref, w_ref, wdt_ref, p_ref, dt_ref, u_ref):
    @pl.when(pl.program_id(1) == 0)
    def _():
        u = _rms(x_ref[...], g_ref[...]).astype(BF16)
        u_ref[...] = u
        dt_ref[...] = _dot(u, wdt_ref[...])

    p_ref[...] = _dot(u_ref[...], w_ref[...]).astype(BF16)


def _inproj(h2d, g, w, wdt, tm, tn):
    t = h2d.shape[0]
    return pl.pallas_call(
        _inproj_kernel,
        grid=(t // tm, P_COLS // tn),
        in_specs=[
            pl.BlockSpec((tm, D_MODEL), lambda i, j: (i, 0)),
            pl.BlockSpec((1, D_MODEL), lambda i, j: (0, 0)),
            pl.BlockSpec((D_MODEL, tn), lambda i, j: (0, j)),
            pl.BlockSpec((D_MODEL, DT_PAD), lambda i, j: (0, 0)),
        ],
        out_specs=[
            pl.BlockSpec((tm, tn), lambda i, j: (i, j)),
            pl.BlockSpec((tm, DT_PAD), lambda i, j: (i, 0)),
        ],
        out_shape=[
            jax.ShapeDtypeStruct((t, P_COLS), BF16),
            jax.ShapeDtypeStruct((t, DT_PAD), F32),
        ],
        scratch_shapes=[pltpu.VMEM((tm, D_MODEL), BF16)],
        compiler_params=_cparams(("parallel", "arbitrary")),
        name="inproj",
    )(h2d, g, w, wdt)


CONV_HALO = 32
CONV_ROWS = 32


def _conv_kernel(cur_ref, halo_ref, gate_ref, dww_ref, dwb_ref, lng_ref, lnb_ref, gb_ref,
                 w_ref, o_ref, buf_ref, acc_ref):
    tl = cur_ref.shape[0]
    c = CONV_DIM
    cur = cur_ref[...].astype(F32)
    buf_ref[CONV_HALO:CONV_HALO + tl, :] = cur[:, :c] * _sigmoid(cur[:, c:])
    hal = halo_ref[...].astype(F32)
    hglu = hal[:, :c] * _sigmoid(hal[:, c:])
    buf_ref[0:CONV_HALO, :] = jnp.where(pl.program_id(1) > 0, hglu, 0.0)

    first = CONV_HALO - (CONV_KERNEL - 1)
    for r in range(0, tl, CONV_ROWS):
        for cb in range(0, c, LANES):
            acc = jnp.broadcast_to(dwb_ref[:, cb:cb + LANES], (CONV_ROWS, LANES))
            for j in range(CONV_KERNEL):
                acc = acc + (dww_ref[j:j + 1, cb:cb + LANES]
                             * buf_ref[first + r + j:first + r + j + CONV_ROWS, cb:cb + LANES])
            acc_ref[r:r + CONV_ROWS, cb:cb + LANES] = acc

    y = acc_ref[...]
    mu = jnp.mean(y, axis=-1, keepdims=True)
    yc = y - mu
    yn = yc * lax.rsqrt(jnp.mean(yc * yc, axis=-1, keepdims=True) + EPS)
    yn = _silu(yn * lng_ref[...] + lnb_ref[...])
    out = _dot(yn.astype(BF16), w_ref[...])
    gate = _sigmoid(gate_ref[...].astype(F32) + gb_ref[...])
    o_ref[...] = (gate * out).astype(BF16)


def _conv_branch(p, dww, dwb, lng, lnb, gb, w, batch, seq, tl):
    nl = seq // tl
    hal_per_tile = tl // CONV_HALO
    cw = 2 * CONV_DIM
    return pl.pallas_call(
        _conv_kernel,
        grid=(batch, nl),
        in_specs=[
            pl.BlockSpec((tl, cw), lambda b, l: (b * nl + l, P_CONV // cw)),
            pl.BlockSpec((CONV_HALO, cw),
                         lambda b, l: (jnp.maximum((b * nl + l) * hal_per_tile - 1, 0), P_CONV // cw)),
            pl.BlockSpec((tl, D_MODEL), lambda b, l: (b * nl + l, P_GATE // D_MODEL + 0)),
            _const_spec((CONV_KERNEL, CONV_DIM)),
            _const_spec((1, CONV_DIM)),
            _const_spec((1, CONV_DIM)),
            _const_spec((1, CONV_DIM)),
            _const_spec((1, D_MODEL)),
            _const_spec((CONV_DIM, D_MODEL)),
        ],
        out_specs=pl.BlockSpec((tl, D_MODEL), lambda b, l: (b * nl + l, 0)),
        out_shape=jax.ShapeDtypeStruct((batch * seq, D_MODEL), BF16),
        scratch_shapes=[pltpu.VMEM((CONV_HALO + tl, CONV_DIM), F32),
                        pltpu.VMEM((tl, CONV_DIM), F32)],
        compiler_params=_cparams(("parallel", "arbitrary")),
        name="conv_branch",
    )(p, p, p, dww, dwb, lng, lnb, gb, w)


SSD_TAIL = SUBLANES


def _ssd_kernel(z_ref, xbc_ref, dt_ref, gate_ref, cw_ref, cb_ref, dtb_ref, a_ref, dexp_ref,
                ng_ref, gb_ref, w_ref, e_ref, tril_ref, o_ref,
                xb_ref, xs_ref, bm_ref, cm_ref, dts_ref, y_ref, st_ref):
    tl = z_ref.shape[0]
    q = SSD_CHUNK
    gw = SSD_GROUP_W

    @pl.when(pl.program_id(1) == 0)
    def _():
        st_ref[...] = jnp.zeros_like(st_ref)
        xb_ref[0:SSD_TAIL, :] = jnp.zeros((SSD_TAIL, SSD_XBC), F32)

    xb_ref[SSD_TAIL:SSD_TAIL + tl, :] = xbc_ref[...].astype(F32)
    first = SSD_TAIL - (SSD_CONV - 1)
    for cbk in range(0, SSD_XBC, 512):
        acc = jnp.broadcast_to(cb_ref[:, cbk:cbk + 512], (tl, 512))
        for j in range(SSD_CONV):
            acc = acc + cw_ref[j:j + 1, cbk:cbk + 512] * xb_ref[first + j:first + j + tl, cbk:cbk + 512]
        act = _silu(acc)
        if cbk < SSD_INNER:
            xs_ref[:, cbk:cbk + 512] = act
        elif cbk < SSD_INNER + SSD_BC:
            bm_ref[...] = act
        else:
            cm_ref[...] = act
    xb_ref[0:SSD_TAIL, :] = xb_ref[tl:tl + SSD_TAIL, :]

    x = dt_ref[...] + dtb_ref[...]
    dts_ref[...] = jnp.maximum(x, 0.0) + jnp.log1p(jnp.exp(-jnp.abs(x)))

    row = lax.broadcasted_iota(jnp.int32, (q, q), 0)
    col = lax.broadcasted_iota(jnp.int32, (q, q), 1)
    tri = row >= col
    lane = lax.broadcasted_iota(jnp.int32, (q, LANES), 1)
    lo_half = lane < SSD_HEAD_DIM

    def chunk(ci, carry):
        r0 = pl.multiple_of(ci * q, q)
        dt_c = dts_ref[pl.ds(r0, q), :]
        da = dt_c * a_ref[...]
        hi, mid, lo = _split3(da)
        tril = tril_ref[...]
        cs = _dot(tril, hi) + _dot(tril, mid) + _dot(tril, lo)
        cs_t = cs.T
        dt_t = dt_c.T
        cs_last = cs[q - 1:q, :]
        w_dec = jnp.exp(cs_last - cs) * dt_c
        dfs = jnp.exp(cs)
        cd = jnp.exp(cs_last)
        cd_hi = cd.astype(BF16)
        cd_lo = (cd - cd_hi.astype(F32)).astype(BF16)
        stack = jnp.concatenate([
            w_dec.astype(BF16), dfs.astype(BF16),
            jnp.broadcast_to(cd_hi, (SUBLANES * 2, LANES)),
            jnp.broadcast_to(cd_lo, (SUBLANES * 2, LANES))], axis=0)
        ex = _dot(stack, e_ref[...])
        w_exp = ex[0:q]
        dfs_exp = ex[q:2 * q]
        cd_exp = ex[2 * q:2 * q + 1] + ex[2 * q + 2 * SUBLANES:2 * q + 2 * SUBLANES + 1]

        xs_c = xs_ref[pl.ds(r0, q), :]
        xs_bf = xs_c.astype(BF16)
        xw = (xs_c * w_exp).astype(BF16)
        for g in range(SSD_GROUPS):
            b_g = bm_ref[pl.ds(r0, q), g * SSD_STATE:(g + 1) * SSD_STATE]
            c_g = cm_ref[pl.ds(r0, q), g * SSD_STATE:(g + 1) * SSD_STATE].astype(BF16)
            cbm = _dot_nt(c_g, b_g.astype(BF16))
            lanes_g = slice(g * gw, (g + 1) * gw)
            st_g = st_ref[:, lanes_g]
            y_g = _dot(c_g, st_g.astype(BF16)) * dfs_exp[:, lanes_g]
            y_ref[pl.ds(r0, q), lanes_g] = y_g
            for pr in range(gw // LANES):
                h0 = (g * gw + pr * LANES) // SSD_HEAD_DIM
                sc = []
                for h in (h0, h0 + 1):
                    diff = cs[:, h:h + 1] - cs_t[h:h + 1, :]
                    dec = jnp.where(tri, jnp.exp(jnp.where(tri, diff, 0.0)), 0.0)
                    sc.append((cbm * dec * dt_t[h:h + 1, :]).astype(BF16))
                lhs = jnp.concatenate(sc, axis=1)
                lanes_p = slice(g * gw + pr * LANES, g * gw + (pr + 1) * LANES)
                xp = xs_bf[:, lanes_p]
                zero = jnp.zeros_like(xp)
                rhs = jnp.concatenate([jnp.where(lo_half, xp, zero),
                                       jnp.where(lo_half, zero, xp)], axis=0)
                y_ref[pl.ds(r0, q), lanes_p] += _dot(lhs, rhs)
            st_ref[:, lanes_g] = st_g * cd_exp[:, lanes_g] + _dot(b_g.T.astype(BF16), xw[:, lanes_g])
        return carry

    lax.fori_loop(0, tl // q, chunk, 0)

    y = y_ref[...] + xs_ref[...] * dexp_ref[...]
    y = y * _silu(z_ref[...].astype(F32))
    parts = []
    for g in range(SSD_GROUPS):
        yg = y[:, g * gw:(g + 1) * gw]
        parts.append(yg * lax.rsqrt(jnp.mean(yg * yg, axis=-1, keepdims=True) + EPS))
    yn = jnp.concatenate(parts, axis=1) * ng_ref[...]
    out = _dot(yn.astype(BF16), w_ref[...])
    gate = _sigmoid(gate_ref[...].astype(F32) + gb_ref[...])
    o_ref[...] = (gate * out).astype(BF16)


def _ssd_branch(p, dt, cw, cb, dtb, a_row, dexp, ng, gb, w, e_mat, tril, batch, seq, tl):
    nl = seq // tl
    return pl.pallas_call(
        _ssd_kernel,
        grid=(batch, nl),
        in_specs=[
            pl.BlockSpec((tl, SSD_INNER), lambda b, l: (b * nl + l, P_Z // SSD_INNER)),
            pl.BlockSpec((tl, SSD_XBC), lambda b, l: (b * nl + l, P_XBC // SSD_XBC)),
            pl.BlockSpec((tl, DT_PAD), lambda b, l: (b * nl + l, 0)),
            pl.BlockSpec((tl, D_MODEL), lambda b, l: (b * nl + l, P_GATE // D_MODEL + 1)),
            _const_spec((SSD_CONV, SSD_XBC)),
            _const_spec((1, SSD_XBC)),
            _const_spec((1, DT_PAD)),
            _const_spec((1, DT_PAD)),
            _const_spec((1, SSD_INNER)),
            _const_spec((1, SSD_INNER)),
            _const_spec((1, D_MODEL)),
            _const_spec((SSD_INNER, D_MODEL)),
            _const_spec((DT_PAD, SSD_INNER)),
            _const_spec((SSD_CHUNK, SSD_CHUNK)),
        ],
        out_specs=pl.BlockSpec((tl, D_MODEL), lambda b, l: (b * nl + l, 0)),
        out_shape=jax.ShapeDtypeStruct((batch * seq, D_MODEL), BF16),
        scratch_shapes=[
            pltpu.VMEM((SSD_TAIL + tl, SSD_XBC), F32),
            pltpu.VMEM((tl, SSD_INNER), F32),
            pltpu.VMEM((tl, SSD_BC), F32),
            pltpu.VMEM((tl, SSD_BC), F32),
            pltpu.VMEM((tl, DT_PAD), F32),
            pltpu.VMEM((tl, SSD_INNER), F32),
            pltpu.VMEM((SSD_STATE, SSD_INNER), F32),
        ],
        compiler_params=_cparams(("parallel", "arbitrary")),
        name="ssd_branch",
    )(p, p, dt, p, cw, cb, dtb, a_row, dexp, ng, gb, w, e_mat, tril)


def _swa_kernel(sink_ref, q_ref, kc_ref, vc_ref, kp_ref, vp_ref, gate_ref, bias_ref, qg_ref,
                kg_ref, eq_ref, ek_ref, rep_ref, gb_ref, w_ref, o_ref, att_ref):
    tq = q_ref.shape[0]
    blk = ATTN_BLOCK
    gwid = ATTN_GROUP_W
    inv_d = 1.0 / ATTN_HEAD_DIM

    qf = q_ref[...].astype(F32)
    q_ss = _dot((qf * qf).astype(BF16), eq_ref[...])
    qn = (qf * lax.rsqrt(q_ss * inv_d + EPS) * qg_ref[...] * (ATTN_HEAD_DIM ** -0.5)).astype(BF16)

    kf = jnp.concatenate([kp_ref[...], kc_ref[...]], axis=0).astype(F32)
    k_ss = _dot((kf * kf).astype(BF16), ek_ref[...])
    kn = (kf * lax.rsqrt(k_ss * inv_d + EPS) * kg_ref[...]).astype(BF16)
    k_rep = _dot(kn, rep_ref[...]).astype(BF16)
    v_all = jnp.concatenate([vp_ref[...], vc_ref[...]], axis=0)
    v_rep = _dot(v_all, rep_ref[...]).astype(BF16)

    lane = lax.broadcasted_iota(jnp.int32, (1, gwid), 1) // ATTN_HEAD_DIM
    key_col = lax.broadcasted_iota(jnp.int32, (blk, 2 * blk), 1)
    first_tile = pl.program_id(1) == 0

    for n in range(tq // blk):
        r0 = n * blk
        for g in range(ATTN_KV_HEADS):
            lanes_g = slice(g * gwid, (g + 1) * gwid)
            q_g = qn[r0:r0 + blk, lanes_g]
            k_g = k_rep[r0:r0 + 2 * blk, lanes_g]
            v_g = v_rep[r0:r0 + 2 * blk, lanes_g]
            probs = []
            v_blocks = []
            for hh in range(ATTN_REP):
                h = g * ATTN_REP + hh
                head = lane == hh
                s = _dot_nt(jnp.where(head, q_g, jnp.zeros_like(q_g)), k_g) + bias_ref[h]
                if n == 0:
                    s = jnp.where(jnp.logical_and(first_tile, key_col < blk), NEG_INF, s)
                sink = sink_ref[h]
                m = jnp.maximum(jnp.max(s, axis=-1, keepdims=True), sink)
                pe = jnp.exp(s - m)
                den = jnp.sum(pe, axis=-1, keepdims=True) + jnp.exp(sink - m)
                probs.append((pe * (1.0 / den)).astype(BF16))
                v_blocks.append(jnp.where(head, v_g, jnp.zeros_like(v_g)))
            att_ref[r0:r0 + blk, lanes_g] = _dot(jnp.concatenate(probs, axis=1),
                                                 jnp.concatenate(v_blocks, axis=0))

    out = _dot(att_ref[...].astype(BF16), w_ref[...])
    gate = _sigmoid(gate_ref[...].astype(F32) + gb_ref[...])
    o_ref[...] = (gate * out).astype(BF16)


def _swa_branch(p, sinks, bias, qg, kg, eq, ek, rep, gb, w, batch, seq, tq):
    nl = seq // tq
    per_tile = tq // ATTN_BLOCK
    kvw = ATTN_KV_DIM

    def prev_blk(b, l):
        return jnp.maximum((b * nl + l) * per_tile - 1, 0)

    return pl.pallas_call(
        _swa_kernel,
        grid=(batch, nl),
        in_specs=[
            pl.BlockSpec(memory_space=pltpu.SMEM),
            pl.BlockSpec((tq, ATTN_DIM), lambda b, l: (b * nl + l, P_Q // ATTN_DIM)),
            pl.BlockSpec((tq, kvw), lambda b, l: (b * nl + l, P_K // kvw)),
            pl.BlockSpec((tq, kvw), lambda b, l: (b * nl + l, P_V // kvw)),
            pl.BlockSpec((ATTN_BLOCK, kvw), lambda b, l: (prev_blk(b, l), P_K // kvw)),
            pl.BlockSpec((ATTN_BLOCK, kvw), lambda b, l: (prev_blk(b, l), P_V // kvw)),
            pl.BlockSpec((tq, D_MODEL), lambda b, l: (b * nl + l, P_GATE // D_MODEL + 2)),
            _const_spec((ATTN_HEADS, ATTN_BLOCK, 2 * ATTN_BLOCK)),
            _const_spec((1, ATTN_DIM)),
            _const_spec((1, kvw)),
            _const_spec((ATTN_DIM, ATTN_DIM)),
            _const_spec((kvw, kvw)),
            _const_spec((kvw, ATTN_DIM)),
            _const_spec((1, D_MODEL)),
            _const_spec((ATTN_DIM, D_MODEL)),
        ],
        out_specs=pl.BlockSpec((tq, D_MODEL), lambda b, l: (b * nl + l, 0)),
        out_shape=jax.ShapeDtypeStruct((batch * seq, D_MODEL), BF16),
        scratch_shapes=[pltpu.VMEM((tq, ATTN_DIM), F32)],
        compiler_params=_cparams(("parallel", "arbitrary")),
        name="swa_branch",
    )(sinks, p, p, p, p, p, p, bias, qg, kg, eq, ek, rep, gb, w)


def _band_buckets():
    qi = np.arange(ATTN_BLOCK)[:, None] + ATTN_BLOCK
    kj = np.arange(2 * ATTN_BLOCK)[None, :]
    dist = qi - kj
    max_exact = REL_BUCKETS // 2
    d = np.maximum(dist, 1).astype(np.float32)
    large = max_exact + (np.log(d / np.float32(max_exact)) / np.float32(math.log(REL_MAX_DIST / max_exact))
                         * np.float32(REL_BUCKETS - max_exact)).astype(np.int32)
    large = np.minimum(large, REL_BUCKETS - 1)
    bucket = np.where(dist < max_exact, np.maximum(dist, 0), large).astype(np.int32)
    in_window = (dist >= 0) & (dist < ATTN_WINDOW)
    return bucket, in_window.astype(np.int32)


def _bias_kernel(tab_ref, bucket_ref, win_ref, o_ref):
    bucket = bucket_ref[...]
    win = win_ref[...] > 0
    for h in range(ATTN_HEADS):
        acc = jnp.zeros(bucket.shape, F32)
        for b in range(REL_BUCKETS):
            acc = jnp.where(bucket == b, tab_ref[b, h], acc)
        o_ref[h] = jnp.where(win, acc, NEG_INF)


def _band_bias(rel_table):
    bucket, win = _band_buckets()
    shape = (ATTN_BLOCK, 2 * ATTN_BLOCK)
    return pl.pallas_call(
        _bias_kernel,
        in_specs=[pl.BlockSpec(memory_space=pltpu.SMEM),
                  pl.BlockSpec(shape, lambda: (0, 0)),
                  pl.BlockSpec(shape, lambda: (0, 0))],
        out_specs=pl.BlockSpec((ATTN_HEADS,) + shape, lambda: (0, 0, 0)),
        out_shape=jax.ShapeDtypeStruct((ATTN_HEADS,) + shape, F32),
        name="band_bias",
    )(rel_table, jnp.asarray(bucket), jnp.asarray(win))


def _memkv_kernel(mem_ref, g_ref, w_ref, kg_ref, k_ref, v_ref):
    mn = _rms(mem_ref[0], g_ref[0]).astype(BF16)
    kv = _dot(mn, w_ref[0])
    parts = []
    for h in range(XATTN_HEADS):
        kh = kv[:, h * XATTN_HEAD_DIM:(h + 1) * XATTN_HEAD_DIM]
        parts.append(_rms(kh, kg_ref[0]))
    k_ref[0, 0] = jnp.concatenate(parts, axis=1).astype(BF16)
    v_ref[0, 0] = kv[:, D_MODEL:].astype(BF16)


def _mem_kv(mem, g, w, kg):
    depth = w.shape[0]
    batch = mem.shape[0]
    out = jax.ShapeDtypeStruct((depth, batch, MEM_LEN, D_MODEL), BF16)
    return pl.pallas_call(
        _memkv_kernel,
        grid=(depth, batch),
        in_specs=[
            pl.BlockSpec((1, MEM_LEN, D_MODEL), lambda i, b: (b, 0, 0)),
            pl.BlockSpec((1, 1, D_MODEL), lambda i, b: (i, 0, 0)),
            pl.BlockSpec((1, D_MODEL, 2 * D_MODEL), lambda i, b: (i, 0, 0)),
            pl.BlockSpec((1, 1, XATTN_HEAD_DIM), lambda i, b: (i, 0, 0)),
        ],
        out_specs=[pl.BlockSpec((1, 1, MEM_LEN, D_MODEL), lambda i, b: (i, b, 0, 0)),
                   pl.BlockSpec((1, 1, MEM_LEN, D_MODEL), lambda i, b: (i, b, 0, 0))],
        out_shape=[out, out],
        compiler_params=_cparams(("arbitrary", "arbitrary")),
        name="mem_kv",
    )(mem, g, w, kg)


def _tail_kernel(h_ref, ya_ref, yb_ref, yc_ref, k_ref, v_ref, wmix_ref, gx_ref, wq_ref, qg_ref,
                 wo_ref, gm_ref, wup_ref, wdn_ref, o_ref, att_ref):
    merged = ya_ref[...].astype(F32) + yb_ref[...].astype(F32) + yc_ref[...].astype(F32)
    h = h_ref[...] + _dot(merged.astype(BF16), wmix_ref[...])

    qx = _dot(_rms(h, gx_ref[...]).astype(BF16), wq_ref[...])
    for hd in range(XATTN_HEADS):
        lanes_h = slice(hd * XATTN_HEAD_DIM, (hd + 1) * XATTN_HEAD_DIM)
        qh = (_rms(qx[:, lanes_h], qg_ref[...]) * (XATTN_HEAD_DIM ** -0.5)).astype(BF16)
        s = _dot_nt(qh, k_ref[0, 0, :, lanes_h])
        m = jnp.max(s, axis=-1, keepdims=True)
        pe = jnp.exp(s - m)
        pr = pe * (1.0 / jnp.sum(pe, axis=-1, keepdims=True))
        att_ref[:, lanes_h] = _dot(pr.astype(BF16), v_ref[0, 0, :, lanes_h])
    h = h + _dot(att_ref[...].astype(BF16), wo_ref[...])

    u = _rms(h, gm_ref[...]).astype(BF16)
    acc = h
    for cb in range(0, MLP_HIDDEN, D_MODEL):
        up = jnp.maximum(_dot(u, wup_ref[:, cb:cb + D_MODEL]), 0.0)
        acc = acc + _dot((up * up).astype(BF16), wdn_ref[cb:cb + D_MODEL, :])
    o_ref[...] = acc


def _tail(h2d, ya, yb, yc, k_mem, v_mem, layer, wmix, gx, wq, qg, wo, gm, wup, wdn, batch, seq, tl):
    nl = seq // tl
    act = lambda b, l: (b * nl + l, 0)
    kv_spec = pl.BlockSpec((1, 1, MEM_LEN, D_MODEL), lambda b, l: (layer, b, 0, 0))
    return pl.pallas_call(
        _tail_kernel,
        grid=(batch, nl),
        in_specs=[
            pl.BlockSpec((tl, D_MODEL), act),
            pl.BlockSpec((tl, D_MODEL), act),
            pl.BlockSpec((tl, D_MODEL), act),
            pl.BlockSpec((tl, D_MODEL), act),
            kv_spec, kv_spec,
            _const_spec((D_MODEL, D_MODEL)),
            _const_spec((1, D_MODEL)),
            _const_spec((D_MODEL, D_MODEL)),
            _const_spec((1, XATTN_HEAD_DIM)),
            _const_spec((D_MODEL, D_MODEL)),
            _const_spec((1, D_MODEL)),
            _const_spec((D_MODEL, MLP_HIDDEN)),
            _const_spec((MLP_HIDDEN, D_MODEL)),
        ],
        out_specs=pl.BlockSpec((tl, D_MODEL), act),
        out_shape=jax.ShapeDtypeStruct((batch * seq, D_MODEL), F32),
        scratch_shapes=[pltpu.VMEM((tl, D_MODEL), F32)],
        compiler_params=_cparams(("parallel", "arbitrary")),
        name="tail",
    )(h2d, ya, yb, yc, k_mem, v_mem, wmix, gx, wq, qg, wo, gm, wup, wdn)


def _head_indicator(width, head_dim):
    idx = np.arange(width) // head_dim
    return (idx[:, None] == idx[None, :]).astype(np.float32)


def _tile_sizes(seq):
    pick = lambda cands: next(c for c in cands if seq % c == 0)
    return dict(tm=pick((512, 256, 128)), tl_conv=pick((256, 128)), tl_ssd=pick((256, 128)),
                tq=pick((256, 128)), tl_tail=pick((512, 256, 128)))


def kernel(x, mem, rel_table, norm_mix, w_in, gate_bias, conv_dw_w, conv_dw_b, conv_ln_g, conv_ln_b, w_conv_out, ssd_conv_w, ssd_conv_b, ssd_dt_bias, ssd_A_log, ssd_D, ssd_norm_g, w_ssd_out, attn_q_norm, attn_k_norm, attn_sinks, w_attn_out, w_mix_out, norm_xattn, norm_mem, w_xq, w_xkv, xattn_q_norm, xattn_k_norm, w_xo, norm_mlp, w_mlp_up, w_mlp_down):
    batch, seq, _ = x.shape
    depth = w_in.shape[0]
    ts = _tile_sizes(seq)
    assert seq % ATTN_BLOCK == 0 and seq % SSD_CHUNK == 0

    w_p = jnp.concatenate([
        w_in[:, :, OFF_GATE:IN_COLS], w_in[:, :, OFF_XBC:OFF_DT], w_in[:, :, OFF_CONV:OFF_Z],
        w_in[:, :, OFF_Z:OFF_XBC], w_in[:, :, OFF_Q:OFF_GATE]], axis=2).astype(BF16)
    w_dt = jnp.pad(w_in[:, :, OFF_DT:OFF_Q], ((0, 0), (0, 0), (0, DT_PAD - SSD_HEADS))).astype(BF16)
    pad_h = ((0, 0), (0, DT_PAD - SSD_HEADS))
    dt_bias = jnp.pad(ssd_dt_bias, pad_h)[:, None, :]
    a_row = -jnp.exp(jnp.pad(ssd_A_log, pad_h))[:, None, :]
    d_exp = jnp.repeat(ssd_D, SSD_HEAD_DIM, axis=1)[:, None, :]

    e_np = np.zeros((DT_PAD, SSD_INNER), np.float32)
    e_np[np.arange(SSD_INNER) // SSD_HEAD_DIM, np.arange(SSD_INNER)] = 1.0
    e_mat = jnp.asarray(e_np, BF16)
    tril = jnp.asarray(np.tril(np.ones((SSD_CHUNK, SSD_CHUNK), np.float32)), BF16)
    eq = jnp.asarray(_head_indicator(ATTN_DIM, ATTN_HEAD_DIM), BF16)
    ek = jnp.asarray(_head_indicator(ATTN_KV_DIM, ATTN_HEAD_DIM), BF16)
    rep_np = np.zeros((ATTN_KV_DIM, ATTN_DIM), np.float32)
    lanes = np.arange(ATTN_DIM)
    rep_np[(lanes // ATTN_GROUP_W) * ATTN_HEAD_DIM + lanes % ATTN_HEAD_DIM, lanes] = 1.0
    rep = jnp.asarray(rep_np, BF16)

    bias = _band_bias(rel_table)
    k_mem, v_mem = _mem_kv(mem, norm_mem[:, None, :], w_xkv.astype(BF16), xattn_k_norm[:, None, :])

    row = lambda a, i: a[i][None, :]
    h = x.reshape(batch * seq, D_MODEL)
    for i in range(depth):
        p, dt = _inproj(h, row(norm_mix, i), w_p[i], w_dt[i], ts["tm"], P_COLS // 4)
        ya = _conv_branch(p, conv_dw_w[i], row(conv_dw_b, i), row(conv_ln_g, i), row(conv_ln_b, i),
                          gate_bias[i, 0][None, :], w_conv_out[i].astype(BF16), batch, seq, ts["tl_conv"])
        yb = _ssd_branch(p, dt, ssd_conv_w[i], row(ssd_conv_b, i), dt_bias[i], a_row[i], d_exp[i],
                         row(ssd_norm_g, i), gate_bias[i, 1][None, :], w_ssd_out[i].astype(BF16),
                         e_mat, tril, batch, seq, ts["tl_ssd"])
        yc = _swa_branch(p, attn_sinks[i], bias, jnp.tile(attn_q_norm[i], ATTN_HEADS)[None, :],
                         jnp.tile(attn_k_norm[i], ATTN_KV_HEADS)[None, :], eq, ek, rep,
                         gate_bias[i, 2][None, :], w_attn_out[i].astype(BF16), batch, seq, ts["tq"])
        h = _tail(h, ya, yb, yc, k_mem, v_mem, i, w_mix_out[i].astype(BF16), row(norm_xattn, i),
                  w_xq[i].astype(BF16), row(xattn_q_norm, i), w_xo[i].astype(BF16), row(norm_mlp, i),
                  w_mlp_up[i].astype(BF16), w_mlp_down[i].astype(BF16), batch, seq, ts["tl_tail"])
    return h.reshape(batch, seq, D_MODEL)
```

```python
import functools
import math

import numpy as np
import jax
import jax.numpy as jnp
from jax import lax
from jax.experimental import pallas as pl
from jax.experimental.pallas import tpu as pltpu

F32 = jnp.float32
BF16 = jnp.bfloat16

D_MODEL = 1024
MEM_LEN = 256
EPS = 1e-6
NEG_INF = -1e30

CONV_DIM = D_MODEL
CONV_KERNEL = 31

SSD_INNER = 2 * D_MODEL
SSD_HEAD_DIM = 64
SSD_HEADS = SSD_INNER // SSD_HEAD_DIM
SSD_GROUPS = 4
SSD_STATE = 128
SSD_CONV = 4
SSD_CHUNK = 128
SSD_BC = SSD_GROUPS * SSD_STATE
SSD_XBC = SSD_INNER + 2 * SSD_BC
SSD_GROUP_W = SSD_INNER // SSD_GROUPS

ATTN_HEADS = 16
ATTN_KV_HEADS = 4
ATTN_HEAD_DIM = 64
ATTN_DIM = ATTN_HEADS * ATTN_HEAD_DIM
ATTN_KV_DIM = ATTN_KV_HEADS * ATTN_HEAD_DIM
ATTN_REP = ATTN_HEADS // ATTN_KV_HEADS
ATTN_WINDOW = 128
ATTN_BLOCK = 128
ATTN_GROUP_W = ATTN_REP * ATTN_HEAD_DIM

REL_BUCKETS = 32
REL_MAX_DIST = 128

XATTN_HEADS = 4
XATTN_HEAD_DIM = D_MODEL // XATTN_HEADS

N_BRANCH = 3
MLP_HIDDEN = 4 * D_MODEL

OFF_CONV = 0
OFF_Z = OFF_CONV + 2 * CONV_DIM
OFF_XBC = OFF_Z + SSD_INNER
OFF_DT = OFF_XBC + SSD_XBC
OFF_Q = OFF_DT + SSD_HEADS
OFF_K = OFF_Q + ATTN_DIM
OFF_V = OFF_K + ATTN_KV_DIM
OFF_GATE = OFF_V + ATTN_KV_DIM
IN_COLS = OFF_GATE + N_BRANCH * D_MODEL

LANES = 128
SUBLANES = 8

P_GATE = 0
P_XBC = P_GATE + N_BRANCH * D_MODEL
P_CONV = P_XBC + SSD_XBC
P_Z = P_CONV + 2 * CONV_DIM
P_Q = P_Z + SSD_INNER
P_K = P_Q + ATTN_DIM
P_V = P_K + ATTN_KV_DIM
P_COLS = P_V + ATTN_KV_DIM
DT_PAD = LANES

VMEM_LIMIT = 56 * 1024 * 1024


def _cparams(semantics):
    return pltpu.CompilerParams(dimension_semantics=semantics, vmem_limit_bytes=VMEM_LIMIT)


def _const_spec(shape):
    nd = len(shape)
    return pl.BlockSpec(shape, lambda *_: (0,) * nd, pipeline_mode=pl.Buffered(1))


def _sigmoid(x):
    return 1.0 / (1.0 + jnp.exp(-x))


def _silu(x):
    return x * _sigmoid(x)


def _rms(x, g):
    return x * lax.rsqrt(jnp.mean(x * x, axis=-1, keepdims=True) + EPS) * g


def _dot(a, b):
    return jnp.dot(a, b, preferred_element_type=F32)


def _dot_nt(a, b):
    return lax.dot_general(a, b, (((1,), (1,)), ((), ())), preferred_element_type=F32)


def _shifted_taps(buf_ref, w_ref, offsets, r, rows, cols):
    acc = None
    for s in range(SUBLANES):
        group = [(j, o) for j, o in enumerate(offsets) if o % SUBLANES == s]
        if not group:
            continue
        n = rows if s == 0 else rows + SUBLANES
        t_s = None
        for j, o in group:
            base = r + o - s
            term = w_ref[j:j + 1, cols] * buf_ref[base:base + n, cols]
            t_s = term if t_s is None else t_s + term
        part = t_s if s == 0 else t_s[s:s + rows]
        acc = part if acc is None else acc + part
    return acc


def _split3(x):
    hi = x.astype(BF16)
    r1 = x - hi.astype(F32)
    mid = r1.astype(BF16)
    lo = (r1 - mid.astype(F32)).astype(BF16)
    return hi, mid, lo


def _inproj_kernel(x_ref, g_ref, w_ref, wdt_ref, p_ref, dt_ref, u_ref):
    @pl.when(pl.program_id(1) == 0)
    def _():
        u = _rms(x_ref[...], g_ref[...]).astype(BF16)
        u_ref[...] = u
        dt_ref[...] = _dot(u, wdt_ref[...])

    p_ref[...] = _dot(u_ref[...], w_ref[...]).astype(BF16)


def _inproj(h2d, g, w, wdt, tm, tn):
    t = h2d.shape[0]
    return pl.pallas_call(
        _inproj_kernel,
        grid=(t // tm, P_COLS // tn),
        in_specs=[
            pl.BlockSpec((tm, D_MODEL), lambda i, j: (i, 0)),
            pl.BlockSpec((1, D_MODEL), lambda i, j: (0, 0)),
            pl.BlockSpec((D_MODEL, tn), lambda i, j: (0, j)),
            pl.BlockSpec((D_MODEL, DT_PAD), lambda i, j: (0, 0)),
        ],
        out_specs=[
            pl.BlockSpec((tm, tn), lambda i, j: (i, j)),
            pl.BlockSpec((tm, DT_PAD), lambda i, j: (i, 0)),
        ],
        out_shape=[
            jax.ShapeDtypeStruct((t, P_COLS), BF16),
            jax.ShapeDtypeStruct((t, DT_PAD), F32),
        ],
        scratch_shapes=[pltpu.VMEM((tm, D_MODEL), BF16)],
        compiler_params=_cparams(("parallel", "arbitrary")),
        name="inproj",
    )(h2d, g, w, wdt)


CONV_HALO = 32
CONV_ROWS = 64


def _conv_kernel(cur_ref, halo_ref, gate_ref, dww_ref, dwb_ref, lng_ref, lnb_ref, gb_ref,
                 w_ref, o_ref, buf_ref, acc_ref):
    tl = cur_ref.shape[0]
    c = CONV_DIM
    cur = cur_ref[...].astype(F32)
    buf_ref[CONV_HALO:CONV_HALO + tl, :] = cur[:, :c] * _sigmoid(cur[:, c:])
    hal = halo_ref[...].astype(F32)
    hglu = hal[:, :c] * _sigmoid(hal[:, c:])
    buf_ref[0:CONV_HALO, :] = jnp.where(pl.program_id(1) > 0, hglu, 0.0)

    first = CONV_HALO - (CONV_KERNEL - 1)
    offsets = [first + j for j in range(CONV_KERNEL)]
    for r in range(0, tl, CONV_ROWS):
        for cb in range(0, c, LANES):
            cols = slice(cb, cb + LANES)
            acc_ref[r:r + CONV_ROWS, cols] = (
                _shifted_taps(buf_ref, dww_ref, offsets, r, CONV_ROWS, cols) + dwb_ref[:, cols])

    y = acc_ref[...]
    mu = jnp.mean(y, axis=-1, keepdims=True)
    yc = y - mu
    yn = yc * lax.rsqrt(jnp.mean(yc * yc, axis=-1, keepdims=True) + EPS)
    yn = _silu(yn * lng_ref[...] + lnb_ref[...])
    out = _dot(yn.astype(BF16), w_ref[...])
    gate = _sigmoid(gate_ref[...].astype(F32) + gb_ref[...])
    o_ref[...] = (gate * out).astype(BF16)


def _conv_branch(p, dww, dwb, lng, lnb, gb, w, batch, seq, tl):
    nl = seq // tl
    hal_per_tile = tl // CONV_HALO
    cw = 2 * CONV_DIM
    return pl.pallas_call(
        _conv_kernel,
        grid=(batch, nl),
        in_specs=[
            pl.BlockSpec((tl, cw), lambda b, l: (b * nl + l, P_CONV // cw)),
            pl.BlockSpec((CONV_HALO, cw),
                         lambda b, l: (jnp.maximum((b * nl + l) * hal_per_tile - 1, 0), P_CONV // cw)),
            pl.BlockSpec((tl, D_MODEL), lambda b, l: (b * nl + l, P_GATE // D_MODEL + 0)),
            _const_spec((CONV_KERNEL, CONV_DIM)),
            _const_spec((1, CONV_DIM)),
            _const_spec((1, CONV_DIM)),
            _const_spec((1, CONV_DIM)),
            _const_spec((1, D_MODEL)),
            _const_spec((CONV_DIM, D_MODEL)),
        ],
        out_specs=pl.BlockSpec((tl, D_MODEL), lambda b, l: (b * nl + l, 0)),
        out_shape=jax.ShapeDtypeStruct((batch * seq, D_MODEL), BF16),
        scratch_shapes=[pltpu.VMEM((CONV_HALO + tl, CONV_DIM), F32),
                        pltpu.VMEM((tl, CONV_DIM), F32)],
        compiler_params=_cparams(("parallel", "arbitrary")),
        name="conv_branch",
    )(p, p, p, dww, dwb, lng, lnb, gb, w)


SSD_TAIL = SUBLANES
SSD_CONV_ROWS = 64
SSD_CONV_COLS = 256


def _ssd_kernel(z_ref, xbc_ref, dt_ref, gate_ref, cw_ref, cb_ref, dtb_ref, a_ref, dexp_ref,
                ng_ref, gb_ref, w_ref, e_ref, tril_ref, o_ref,
                xb_ref, xs_ref, bm_ref, cm_ref, dts_ref, y_ref, st_ref):
    tl = z_ref.shape[0]
    q = SSD_CHUNK
    gw = SSD_GROUP_W

    @pl.when(pl.program_id(1) == 0)
    def _():
        st_ref[...] = jnp.zeros_like(st_ref)
        xb_ref[0:SSD_TAIL, :] = jnp.zeros((SSD_TAIL, SSD_XBC), F32)

    xb_ref[SSD_TAIL:SSD_TAIL + tl, :] = xbc_ref[...].astype(F32)
    first = SSD_TAIL - (SSD_CONV - 1)
    offsets = [first + j for j in range(SSD_CONV)]
    for r in range(0, tl, SSD_CONV_ROWS):
        rows = slice(r, r + SSD_CONV_ROWS)
        for cbk in range(0, SSD_XBC, SSD_CONV_COLS):
            cols = slice(cbk, cbk + SSD_CONV_COLS)
            act = _silu(_shifted_taps(xb_ref, cw_ref, offsets, r, SSD_CONV_ROWS, cols) + cb_ref[:, cols])
            if cbk < SSD_INNER:
                xs_ref[rows, cols] = act
            elif cbk < SSD_INNER + SSD_BC:
                bm_ref[rows, cbk - SSD_INNER:cbk - SSD_INNER + SSD_CONV_COLS] = act
            else:
                off = cbk - SSD_INNER - SSD_BC
                cm_ref[rows, off:off + SSD_CONV_COLS] = act
    xb_ref[0:SSD_TAIL, :] = xb_ref[tl:tl + SSD_TAIL, :]

    x = dt_ref[...] + dtb_ref[...]
    dts_ref[...] = jnp.maximum(x, 0.0) + jnp.log1p(jnp.exp(-jnp.abs(x)))

    row = lax.broadcasted_iota(jnp.int32, (q, q), 0)
    col = lax.broadcasted_iota(jnp.int32, (q, q), 1)
    tri = row >= col
    lane = lax.broadcasted_iota(jnp.int32, (q, LANES), 1)
    lo_half = lane < SSD_HEAD_DIM

    def chunk(ci, carry):
        r0 = pl.multiple_of(ci * q, q)
        dt_c = dts_ref[pl.ds(r0, q), :]
        da = dt_c * a_ref[...]
        hi, mid, lo = _split3(da)
        tril = tril_ref[...]
        cs = _dot(tril, hi) + _dot(tril, mid) + _dot(tril, lo)
        cs_t = cs.T
        dt_t = dt_c.T
        cs_last = cs[q - 1:q, :]
        w_dec = jnp.exp(cs_last - cs) * dt_c
        dfs = jnp.exp(cs)
        cd = jnp.exp(cs_last)
        cd_hi = cd.astype(BF16)
        cd_lo = (cd - cd_hi.astype(F32)).astype(BF16)
        stack = jnp.concatenate([
            w_dec.astype(BF16), dfs.astype(BF16),
            jnp.broadcast_to(cd_hi, (SUBLANES * 2, LANES)),
            jnp.broadcast_to(cd_lo, (SUBLANES * 2, LANES))], axis=0)
        ex = _dot(stack, e_ref[...])
        w_exp = ex[0:q]
        dfs_exp = ex[q:2 * q]
        cd_exp = ex[2 * q:2 * q + 1] + ex[2 * q + 2 * SUBLANES:2 * q + 2 * SUBLANES + 1]

        xs_c = xs_ref[pl.ds(r0, q), :]
        xs_bf = xs_c.astype(BF16)
        xw = (xs_c * w_exp).astype(BF16)
        for g in range(SSD_GROUPS):
            b_g = bm_ref[pl.ds(r0, q), g * SSD_STATE:(g + 1) * SSD_STATE]
            c_g = cm_ref[pl.ds(r0, q), g * SSD_STATE:(g + 1) * SSD_STATE].astype(BF16)
            cbm = _dot_nt(c_g, b_g.astype(BF16))
            lanes_g = slice(g * gw, (g + 1) * gw)
            st_g = st_ref[:, lanes_g]
            y_g = _dot(c_g, st_g.astype(BF16)) * dfs_exp[:, lanes_g]
            y_ref[pl.ds(r0, q), lanes_g] = y_g
            for pr in range(gw // LANES):
                h0 = (g * gw + pr * LANES) // SSD_HEAD_DIM
                sc = []
                for h in (h0, h0 + 1):
                    diff = cs[:, h:h + 1] - cs_t[h:h + 1, :]
                    dec = jnp.where(tri, jnp.exp(jnp.where(tri, diff, 0.0)), 0.0)
                    sc.append((cbm * dec * dt_t[h:h + 1, :]).astype(BF16))
                lhs = jnp.concatenate(sc, axis=1)
                lanes_p = slice(g * gw + pr * LANES, g * gw + (pr + 1) * LANES)
                xp = xs_bf[:, lanes_p]
                zero = jnp.zeros_like(xp)
                rhs = jnp.concatenate([jnp.where(lo_half, xp, zero),
                                       jnp.where(lo_half, zero, xp)], axis=0)
                y_ref[pl.ds(r0, q), lanes_p] += _dot(lhs, rhs)
            st_ref[:, lanes_g] = st_g * cd_exp[:, lanes_g] + _dot(b_g.T.astype(BF16), xw[:, lanes_g])
        return carry

    lax.fori_loop(0, tl // q, chunk, 0)

    y = y_ref[...] + xs_ref[...] * dexp_ref[...]
    y = y * _silu(z_ref[...].astype(F32))
    parts = []
    for g in range(SSD_GROUPS):
        yg = y[:, g * gw:(g + 1) * gw]
        parts.append(yg * lax.rsqrt(jnp.mean(yg * yg, axis=-1, keepdims=True) + EPS))
    yn = jnp.concatenate(parts, axis=1) * ng_ref[...]
    out = _dot(yn.astype(BF16), w_ref[...])
    gate = _sigmoid(gate_ref[...].astype(F32) + gb_ref[...])
    o_ref[...] = (gate * out).astype(BF16)


def _ssd_branch(p, dt, cw, cb, dtb, a_row, dexp, ng, gb, w, e_mat, tril, batch, seq, tl):
    nl = seq // tl
    return pl.pallas_call(
        _ssd_kernel,
        grid=(batch, nl),
        in_specs=[
            pl.BlockSpec((tl, SSD_INNER), lambda b, l: (b * nl + l, P_Z // SSD_INNER)),
            pl.BlockSpec((tl, SSD_XBC), lambda b, l: (b * nl + l, P_XBC // SSD_XBC)),
            pl.BlockSpec((tl, DT_PAD), lambda b, l: (b * nl + l, 0)),
            pl.BlockSpec((tl, D_MODEL), lambda b, l: (b * nl + l, P_GATE // D_MODEL + 1)),
            _const_spec((SSD_CONV, SSD_XBC)),
            _const_spec((1, SSD_XBC)),
            _const_spec((1, DT_PAD)),
            _const_spec((1, DT_PAD)),
            _const_spec((1, SSD_INNER)),
            _const_spec((1, SSD_INNER)),
            _const_spec((1, D_MODEL)),
            _const_spec((SSD_INNER, D_MODEL)),
            _const_spec((DT_PAD, SSD_INNER)),
            _const_spec((SSD_CHUNK, SSD_CHUNK)),
        ],
        out_specs=pl.BlockSpec((tl, D_MODEL), lambda b, l: (b * nl + l, 0)),
        out_shape=jax.ShapeDtypeStruct((batch * seq, D_MODEL), BF16),
        scratch_shapes=[
            pltpu.VMEM((SSD_TAIL + tl, SSD_XBC), F32),
            pltpu.VMEM((tl, SSD_INNER), F32),
            pltpu.VMEM((tl, SSD_BC), F32),
            pltpu.VMEM((tl, SSD_BC), F32),
            pltpu.VMEM((tl, DT_PAD), F32),
            pltpu.VMEM((tl, SSD_INNER), F32),
            pltpu.VMEM((SSD_STATE, SSD_INNER), F32),
        ],
        compiler_params=_cparams(("parallel", "arbitrary")),
        name="ssd_branch",
    )(p, p, dt, p, cw, cb, dtb, a_row, dexp, ng, gb, w, e_mat, tril)


def _swa_kernel(sink_ref, q_ref, kc_ref, vc_ref, kp_ref, vp_ref, gate_ref, bias_ref, qg_ref,
                kg_ref, ek_ref, rep_ref, gb_ref, w_ref, o_ref, att_ref):
    tq = q_ref.shape[0]
    blk = ATTN_BLOCK
    gwid = ATTN_GROUP_W
    inv_d = 1.0 / ATTN_HEAD_DIM

    qf = q_ref[...].astype(F32)
    q_sq = (qf * qf).astype(BF16)
    q_ss = jnp.concatenate([_dot(q_sq[:, g * gwid:(g + 1) * gwid], ek_ref[...])
                            for g in range(ATTN_KV_HEADS)], axis=1)
    qn = (qf * lax.rsqrt(q_ss * inv_d + EPS) * qg_ref[...] * (ATTN_HEAD_DIM ** -0.5)).astype(BF16)

    kf = jnp.concatenate([kp_ref[...], kc_ref[...]], axis=0).astype(F32)
    k_ss = _dot((kf * kf).astype(BF16), ek_ref[...])
    kn = (kf * lax.rsqrt(k_ss * inv_d + EPS) * kg_ref[...]).astype(BF16)
    k_rep = _dot(kn, rep_ref[...]).astype(BF16)
    v_all = jnp.concatenate([vp_ref[...], vc_ref[...]], axis=0)
    v_rep = _dot(v_all, rep_ref[...]).astype(BF16)

    lane = lax.broadcasted_iota(jnp.int32, (1, gwid), 1) // ATTN_HEAD_DIM
    key_col = lax.broadcasted_iota(jnp.int32, (blk, 2 * blk), 1)
    first_tile = pl.program_id(1) == 0

    for n in range(tq // blk):
        r0 = n * blk
        for g in range(ATTN_KV_HEADS):
            lanes_g = slice(g * gwid, (g + 1) * gwid)
            q_g = qn[r0:r0 + blk, lanes_g]
            k_g = k_rep[r0:r0 + 2 * blk, lanes_g]
            v_g = v_rep[r0:r0 + 2 * blk, lanes_g]
            probs = []
            v_blocks = []
            for hh in range(ATTN_REP):
                h = g * ATTN_REP + hh
                head = lane == hh
                s = _dot_nt(jnp.where(head, q_g, jnp.zeros_like(q_g)), k_g) + bias_ref[h]
                if n == 0:
                    s = jnp.where(jnp.logical_and(first_tile, key_col < blk), NEG_INF, s)
                sink = sink_ref[h]
                m = jnp.maximum(jnp.max(s, axis=-1, keepdims=True), sink)
                pe = jnp.exp(s - m)
                den = jnp.sum(pe, axis=-1, keepdims=True) + jnp.exp(sink - m)
                probs.append((pe * (1.0 / den)).astype(BF16))
                v_blocks.append(jnp.where(head, v_g, jnp.zeros_like(v_g)))
            att_ref[r0:r0 + blk, lanes_g] = _dot(jnp.concatenate(probs, axis=1),
                                                 jnp.concatenate(v_blocks, axis=0))

    out = _dot(att_ref[...].astype(BF16), w_ref[...])
    gate = _sigmoid(gate_ref[...].astype(F32) + gb_ref[...])
    o_ref[...] = (gate * out).astype(BF16)


def _swa_branch(p, sinks, bias, qg, kg, ek, rep, gb, w, batch, seq, tq):
    nl = seq // tq
    per_tile = tq // ATTN_BLOCK
    kvw = ATTN_KV_DIM

    def prev_blk(b, l):
        return jnp.maximum((b * nl + l) * per_tile - 1, 0)

    return pl.pallas_call(
        _swa_kernel,
        grid=(batch, nl),
        in_specs=[
            pl.BlockSpec(memory_space=pltpu.SMEM),
            pl.BlockSpec((tq, ATTN_DIM), lambda b, l: (b * nl + l, P_Q // ATTN_DIM)),
            pl.BlockSpec((tq, kvw), lambda b, l: (b * nl + l, P_K // kvw)),
            pl.BlockSpec((tq, kvw), lambda b, l: (b * nl + l, P_V // kvw)),
            pl.BlockSpec((ATTN_BLOCK, kvw), lambda b, l: (prev_blk(b, l), P_K // kvw)),
            pl.BlockSpec((ATTN_BLOCK, kvw), lambda b, l: (prev_blk(b, l), P_V // kvw)),
            pl.BlockSpec((tq, D_MODEL), lambda b, l: (b * nl + l, P_GATE // D_MODEL + 2)),
            _const_spec((ATTN_HEADS, ATTN_BLOCK, 2 * ATTN_BLOCK)),
            _const_spec((1, ATTN_DIM)),
            _const_spec((1, kvw)),
            _const_spec((kvw, kvw)),
            _const_spec((kvw, ATTN_DIM)),
            _const_spec((1, D_MODEL)),
            _const_spec((ATTN_DIM, D_MODEL)),
        ],
        out_specs=pl.BlockSpec((tq, D_MODEL), lambda b, l: (b * nl + l, 0)),
        out_shape=jax.ShapeDtypeStruct((batch * seq, D_MODEL), BF16),
        scratch_shapes=[pltpu.VMEM((tq, ATTN_DIM), F32)],
        compiler_params=_cparams(("parallel", "arbitrary")),
        name="swa_branch",
    )(sinks, p, p, p, p, p, p, bias, qg, kg, ek, rep, gb, w)


def _band_buckets():
    qi = np.arange(ATTN_BLOCK)[:, None] + ATTN_BLOCK
    kj = np.arange(2 * ATTN_BLOCK)[None, :]
    dist = qi - kj
    max_exact = REL_BUCKETS // 2
    d = np.maximum(dist, 1).astype(np.float32)
    large = max_exact + (np.log(d / np.float32(max_exact)) / np.float32(math.log(REL_MAX_DIST / max_exact))
                         * np.float32(REL_BUCKETS - max_exact)).astype(np.int32)
    large = np.minimum(large, REL_BUCKETS - 1)
    bucket = np.where(dist < max_exact, np.maximum(dist, 0), large).astype(np.int32)
    in_window = (dist >= 0) & (dist < ATTN_WINDOW)
    return bucket, in_window.astype(np.int32)


def _bias_kernel(tab_ref, bucket_ref, win_ref, o_ref):
    bucket = bucket_ref[...]
    win = win_ref[...] > 0
    for h in range(ATTN_HEADS):
        acc = jnp.zeros(bucket.shape, F32)
        for b in range(REL_BUCKETS):
            acc = jnp.where(bucket == b, tab_ref[b, h], acc)
        o_ref[h] = jnp.where(win, acc, NEG_INF)


def _band_bias(rel_table):
    bucket, win = _band_buckets()
    shape = (ATTN_BLOCK, 2 * ATTN_BLOCK)
    return pl.pallas_call(
        _bias_kernel,
        in_specs=[pl.BlockSpec(memory_space=pltpu.SMEM),
                  pl.BlockSpec(shape, lambda: (0, 0)),
                  pl.BlockSpec(shape, lambda: (0, 0))],
        out_specs=pl.BlockSpec((ATTN_HEADS,) + shape, lambda: (0, 0, 0)),
        out_shape=jax.ShapeDtypeStruct((ATTN_HEADS,) + shape, F32),
        name="band_bias",
    )(rel_table, jnp.asarray(bucket), jnp.asarray(win))


def _memkv_kernel(mem_ref, g_ref, w_ref, kg_ref, k_ref, v_ref):
    mn = _rms(mem_ref[0], g_ref[0]).astype(BF16)
    kv = _dot(mn, w_ref[0])
    parts = []
    for h in range(XATTN_HEADS):
        kh = kv[:, h * XATTN_HEAD_DIM:(h + 1) * XATTN_HEAD_DIM]
        parts.append(_rms(kh, kg_ref[0]))
    k_ref[0, 0] = jnp.concatenate(parts, axis=1).astype(BF16)
    v_ref[0, 0] = kv[:, D_MODEL:].astype(BF16)


def _mem_kv(mem, g, w, kg):
    depth = w.shape[0]
    batch = mem.shape[0]
    out = jax.ShapeDtypeStruct((depth, batch, MEM_LEN, D_MODEL), BF16)
    return pl.pallas_call(
        _memkv_kernel,
        grid=(depth, batch),
        in_specs=[
            pl.BlockSpec((1, MEM_LEN, D_MODEL), lambda i, b: (b, 0, 0)),
            pl.BlockSpec((1, 1, D_MODEL), lambda i, b: (i, 0, 0)),
            pl.BlockSpec((1, D_MODEL, 2 * D_MODEL), lambda i, b: (i, 0, 0)),
            pl.BlockSpec((1, 1, XATTN_HEAD_DIM), lambda i, b: (i, 0, 0)),
        ],
        out_specs=[pl.BlockSpec((1, 1, MEM_LEN, D_MODEL), lambda i, b: (i, b, 0, 0)),
                   pl.BlockSpec((1, 1, MEM_LEN, D_MODEL), lambda i, b: (i, b, 0, 0))],
        out_shape=[out, out],
        compiler_params=_cparams(("arbitrary", "arbitrary")),
        name="mem_kv",
    )(mem, g, w, kg)


def _tail_kernel(h_ref, ya_ref, yb_ref, yc_ref, k_ref, v_ref, wmix_ref, gx_ref, wq_ref, qg_ref,
                 wo_ref, gm_ref, wup_ref, wdn_ref, o_ref, att_ref):
    merged = ya_ref[...].astype(F32) + yb_ref[...].astype(F32) + yc_ref[...].astype(F32)
    h = h_ref[...] + _dot(merged.astype(BF16), wmix_ref[...])

    qx = _dot(_rms(h, gx_ref[...]).astype(BF16), wq_ref[...])
    for hd in range(XATTN_HEADS):
        lanes_h = slice(hd * XATTN_HEAD_DIM, (hd + 1) * XATTN_HEAD_DIM)
        qh = (_rms(qx[:, lanes_h], qg_ref[...]) * (XATTN_HEAD_DIM ** -0.5)).astype(BF16)
        s = _dot_nt(qh, k_ref[0, 0, :, lanes_h])
        m = jnp.max(s, axis=-1, keepdims=True)
        pe = jnp.exp(s - m)
        pr = pe * (1.0 / jnp.sum(pe, axis=-1, keepdims=True))
        att_ref[:, lanes_h] = _dot(pr.astype(BF16), v_ref[0, 0, :, lanes_h])
    h = h + _dot(att_ref[...].astype(BF16), wo_ref[...])

    u = _rms(h, gm_ref[...]).astype(BF16)
    acc = h
    for cb in range(0, MLP_HIDDEN, D_MODEL):
        up = jnp.maximum(_dot(u, wup_ref[:, cb:cb + D_MODEL]), 0.0)
        acc = acc + _dot((up * up).astype(BF16), wdn_ref[cb:cb + D_MODEL, :])
    o_ref[...] = acc


def _tail(h2d, ya, yb, yc, k_mem, v_mem, layer, wmix, gx, wq, qg, wo, gm, wup, wdn, batch, seq, tl):
    nl = seq // tl
    act = lambda b, l: (b * nl + l, 0)
    kv_spec = pl.BlockSpec((1, 1, MEM_LEN, D_MODEL), lambda b, l: (layer, b, 0, 0))
    return pl.pallas_call(
        _tail_kernel,
        grid=(batch, nl),
        in_specs=[
            pl.BlockSpec((tl, D_MODEL), act),
            pl.BlockSpec((tl, D_MODEL), act),
            pl.BlockSpec((tl, D_MODEL), act),
            pl.BlockSpec((tl, D_MODEL), act),
            kv_spec, kv_spec,
            _const_spec((D_MODEL, D_MODEL)),
            _const_spec((1, D_MODEL)),
            _const_spec((D_MODEL, D_MODEL)),
            _const_spec((1, XATTN_HEAD_DIM)),
            _const_spec((D_MODEL, D_MODEL)),
            _const_spec((1, D_MODEL)),
            _const_spec((D_MODEL, MLP_HIDDEN)),
            _const_spec((MLP_HIDDEN, D_MODEL)),
        ],
        out_specs=pl.BlockSpec((tl, D_MODEL), act),
        out_shape=jax.ShapeDtypeStruct((batch * seq, D_MODEL), F32),
        scratch_shapes=[pltpu.VMEM((tl, D_MODEL), F32)],
        compiler_params=_cparams(("parallel", "arbitrary")),
        name="tail",
    )(h2d, ya, yb, yc, k_mem, v_mem, wmix, gx, wq, qg, wo, gm, wup, wdn)


def _head_indicator(width, head_dim):
    idx = np.arange(width) // head_dim
    return (idx[:, None] == idx[None, :]).astype(np.float32)


def _tile_sizes(seq):
    pick = lambda cands: next(c for c in cands if seq % c == 0)
    return dict(tm=pick((512, 256, 128)), tl_conv=pick((256, 128)), tl_ssd=pick((256, 128)),
                tq=pick((256, 128)), tl_tail=pick((512, 256, 128)))


def kernel(x, mem, rel_table, norm_mix, w_in, gate_bias, conv_dw_w, conv_dw_b, conv_ln_g, conv_ln_b, w_conv_out, ssd_conv_w, ssd_conv_b, ssd_dt_bias, ssd_A_log, ssd_D, ssd_norm_g, w_ssd_out, attn_q_norm, attn_k_norm, attn_sinks, w_attn_out, w_mix_out, norm_xattn, norm_mem, w_xq, w_xkv, xattn_q_norm, xattn_k_norm, w_xo, norm_mlp, w_mlp_up, w_mlp_down):
    batch, seq, _ = x.shape
    depth = w_in.shape[0]
    ts = _tile_sizes(seq)
    assert seq % ATTN_BLOCK == 0 and seq % SSD_CHUNK == 0

    w_p = jnp.concatenate([
        w_in[:, :, OFF_GATE:IN_COLS], w_in[:, :, OFF_XBC:OFF_DT], w_in[:, :, OFF_CONV:OFF_Z],
        w_in[:, :, OFF_Z:OFF_XBC], w_in[:, :, OFF_Q:OFF_GATE]], axis=2).astype(BF16)
    w_dt = jnp.pad(w_in[:, :, OFF_DT:OFF_Q], ((0, 0), (0, 0), (0, DT_PAD - SSD_HEADS))).astype(BF16)
    pad_h = ((0, 0), (0, DT_PAD - SSD_HEADS))
    dt_bias = jnp.pad(ssd_dt_bias, pad_h)[:, None, :]
    a_row = -jnp.exp(jnp.pad(ssd_A_log, pad_h))[:, None, :]
    d_exp = jnp.repeat(ssd_D, SSD_HEAD_DIM, axis=1)[:, None, :]

    e_np = np.zeros((DT_PAD, SSD_INNER), np.float32)
    e_np[np.arange(SSD_INNER) // SSD_HEAD_DIM, np.arange(SSD_INNER)] = 1.0
    e_mat = jnp.asarray(e_np, BF16)
    tril = jnp.asarray(np.tril(np.ones((SSD_CHUNK, SSD_CHUNK), np.float32)), BF16)
    ek = jnp.asarray(_head_indicator(ATTN_KV_DIM, ATTN_HEAD_DIM), BF16)
    rep_np = np.zeros((ATTN_KV_DIM, ATTN_DIM), np.float32)
    lanes = np.arange(ATTN_DIM)
    rep_np[(lanes // ATTN_GROUP_W) * ATTN_HEAD_DIM + lanes % ATTN_HEAD_DIM, lanes] = 1.0
    rep = jnp.asarray(rep_np, BF16)

    bias = _band_bias(rel_table)
    k_mem, v_mem = _mem_kv(mem, norm_mem[:, None, :], w_xkv.astype(BF16), xattn_k_norm[:, None, :])

    row = lambda a, i: a[i][None, :]
    h = x.reshape(batch * seq, D_MODEL)
    for i in range(depth):
        p, dt = _inproj(h, row(norm_mix, i), w_p[i], w_dt[i], ts["tm"], P_COLS // 4)
        ya = _conv_branch(p, conv_dw_w[i], row(conv_dw_b, i), row(conv_ln_g, i), row(conv_ln_b, i),
                          gate_bias[i, 0][None, :], w_conv_out[i].astype(BF16), batch, seq, ts["tl_conv"])
        yb = _ssd_branch(p, dt, ssd_conv_w[i], row(ssd_conv_b, i), dt_bias[i], a_row[i], d_exp[i],
                         row(ssd_norm_g, i), gate_bias[i, 1][None, :], w_ssd_out[i].astype(BF16),
                         e_mat, tril, batch, seq, ts["tl_ssd"])
        yc = _swa_branch(p, attn_sinks[i], bias, jnp.tile(attn_q_norm[i], ATTN_HEADS)[None, :],
                         jnp.tile(attn_k_norm[i], ATTN_KV_HEADS)[None, :], ek, rep,
                         gate_bias[i, 2][None, :], w_attn_out[i].astype(BF16), batch, seq, ts["tq"])
        h = _tail(h, ya, yb, yc, k_mem, v_mem, i, w_mix_out[i].astype(BF16), row(norm_xattn, i),
                  w_xq[i].astype(BF16), row(xattn_q_norm, i), w_xo[i].astype(BF16), row(norm_mlp, i),
                  w_mlp_up[i].astype(BF16), w_mlp_down[i].astype(BF16), batch, seq, ts["tl_tail"])
    return h.reshape(batch, seq, D_MODEL)
```

```python
import math

import numpy as np
import jax
import jax.numpy as jnp
from jax import lax
from jax.experimental import pallas as pl
from jax.experimental.pallas import tpu as pltpu

F32 = jnp.float32
BF16 = jnp.bfloat16

D_MODEL = 1024
MEM_LEN = 256
EPS = 1e-6
NEG_INF = -1e30

CONV_DIM = D_MODEL
CONV_KERNEL = 31

SSD_INNER = 2 * D_MODEL
SSD_HEAD_DIM = 64
SSD_HEADS = SSD_INNER // SSD_HEAD_DIM
SSD_GROUPS = 4
SSD_STATE = 128
SSD_CONV = 4
SSD_CHUNK = 128
SSD_BC = SSD_GROUPS * SSD_STATE
SSD_XBC = SSD_INNER + 2 * SSD_BC
SSD_GROUP_W = SSD_INNER // SSD_GROUPS

ATTN_HEADS = 16
ATTN_KV_HEADS = 4
ATTN_HEAD_DIM = 64
ATTN_DIM = ATTN_HEADS * ATTN_HEAD_DIM
ATTN_KV_DIM = ATTN_KV_HEADS * ATTN_HEAD_DIM
ATTN_REP = ATTN_HEADS // ATTN_KV_HEADS
ATTN_WINDOW = 128
ATTN_BLOCK = 128
ATTN_GROUP_W = ATTN_REP * ATTN_HEAD_DIM

REL_BUCKETS = 32
REL_MAX_DIST = 128

XATTN_HEADS = 4
XATTN_HEAD_DIM = D_MODEL // XATTN_HEADS

N_BRANCH = 3
MLP_HIDDEN = 4 * D_MODEL

OFF_CONV = 0
OFF_Z = OFF_CONV + 2 * CONV_DIM
OFF_XBC = OFF_Z + SSD_INNER
OFF_DT = OFF_XBC + SSD_XBC
OFF_Q = OFF_DT + SSD_HEADS
OFF_K = OFF_Q + ATTN_DIM
OFF_V = OFF_K + ATTN_KV_DIM
OFF_GATE = OFF_V + ATTN_KV_DIM
IN_COLS = OFF_GATE + N_BRANCH * D_MODEL

LANES = 128
SUBLANES = 8

P_GATE = 0
P_XBC = P_GATE + N_BRANCH * D_MODEL
P_CONV = P_XBC + SSD_XBC
P_Z = P_CONV + 2 * CONV_DIM
P_Q = P_Z + SSD_INNER
P_K = P_Q + ATTN_DIM
P_V = P_K + ATTN_KV_DIM
P_COLS = P_V + ATTN_KV_DIM
DT_PAD = LANES

VMEM_LIMIT = 56 * 1024 * 1024


def _cparams(semantics):
    return pltpu.CompilerParams(dimension_semantics=semantics, vmem_limit_bytes=VMEM_LIMIT)


def _const_spec(shape):
    nd = len(shape)
    return pl.BlockSpec(shape, lambda *_: (0,) * nd, pipeline_mode=pl.Buffered(1))


def _sigmoid(x):
    return 1.0 / (1.0 + jnp.exp(-x))


def _silu(x):
    return x * _sigmoid(x)


def _rms(x, g):
    return x * lax.rsqrt(jnp.mean(x * x, axis=-1, keepdims=True) + EPS) * g


def _dot(a, b):
    return jnp.dot(a, b, preferred_element_type=F32)


def _dot_nt(a, b):
    return lax.dot_general(a, b, (((1,), (1,)), ((), ())), preferred_element_type=F32)


def _shifted_taps(buf_ref, w_ref, offsets, r, rows, cols):
    acc = None
    for s in range(SUBLANES):
        group = [(j, o) for j, o in enumerate(offsets) if o % SUBLANES == s]
        if not group:
            continue
        n = rows if s == 0 else rows + SUBLANES
        t_s = None
        for j, o in group:
            base = r + o - s
            term = w_ref[j:j + 1, cols] * buf_ref[base:base + n, cols]
            t_s = term if t_s is None else t_s + term
        part = t_s if s == 0 else t_s[s:s + rows]
        acc = part if acc is None else acc + part
    return acc


def _split3(x):
    hi = x.astype(BF16)
    r1 = x - hi.astype(F32)
    mid = r1.astype(BF16)
    lo = (r1 - mid.astype(F32)).astype(BF16)
    return hi, mid, lo


def _norm_dt_kernel(x_ref, g_ref, wdt_ref, u_ref, dt_ref):
    u = _rms(x_ref[...], g_ref[...]).astype(BF16)
    u_ref[...] = u
    dt_ref[...] = _dot(u, wdt_ref[...])


def _norm_dt(h2d, g, wdt, tm):
    t = h2d.shape[0]
    return pl.pallas_call(
        _norm_dt_kernel,
        grid=(t // tm,),
        in_specs=[pl.BlockSpec((tm, D_MODEL), lambda i: (i, 0)),
                  _const_spec((1, D_MODEL)),
                  _const_spec((D_MODEL, DT_PAD))],
        out_specs=[pl.BlockSpec((tm, D_MODEL), lambda i: (i, 0)),
                   pl.BlockSpec((tm, DT_PAD), lambda i: (i, 0))],
        out_shape=[jax.ShapeDtypeStruct((t, D_MODEL), BF16),
                   jax.ShapeDtypeStruct((t, DT_PAD), F32)],
        compiler_params=_cparams(("parallel",)),
        name="norm_dt",
    )(h2d, g, wdt)


def _inproj_kernel(u_ref, w_ref, p_ref):
    p_ref[...] = _dot(u_ref[...], w_ref[...]).astype(BF16)


def _inproj(u, w, tm, tn):
    t = u.shape[0]
    return pl.pallas_call(
        _inproj_kernel,
        grid=(t // tm, P_COLS // tn),
        in_specs=[pl.BlockSpec((tm, D_MODEL), lambda i, j: (i, 0)),
                  pl.BlockSpec((D_MODEL, tn), lambda i, j: (0, j))],
        out_specs=pl.BlockSpec((tm, tn), lambda i, j: (i, j)),
        out_shape=jax.ShapeDtypeStruct((t, P_COLS), BF16),
        compiler_params=_cparams(("parallel", "arbitrary")),
        name="inproj",
    )(u, w)


CONV_HALO = 32
CONV_ROWS = 64
SSD_TAIL = SUBLANES
SSD_CONV_ROWS = 64
SSD_CONV_COLS = 256


def _round_robin(*streams):
    total = max(len(s) for s in streams)
    order = []
    for step in range(total):
        for s in streams:
            lo = step * len(s) // total
            hi = (step + 1) * len(s) // total
            order.extend(s[lo:hi])
    return order


def _gate(gates_ref, gb_ref, branch):
    cols = slice(branch * D_MODEL, (branch + 1) * D_MODEL)
    return _sigmoid(gates_ref[:, cols].astype(F32) + gb_ref[branch:branch + 1, :])


def _conv_part(cur_ref, halo_ref, dww_ref, dwb_ref, lng_ref, lnb_ref, w_ref, buf_ref, acc_ref):
    tl = cur_ref.shape[0]
    c = CONV_DIM
    cur = cur_ref[...].astype(F32)
    buf_ref[CONV_HALO:CONV_HALO + tl, :] = cur[:, :c] * _sigmoid(cur[:, c:])
    hal = halo_ref[...].astype(F32)
    hglu = hal[:, :c] * _sigmoid(hal[:, c:])
    buf_ref[0:CONV_HALO, :] = jnp.where(pl.program_id(1) > 0, hglu, 0.0)

    first = CONV_HALO - (CONV_KERNEL - 1)
    offsets = [first + j for j in range(CONV_KERNEL)]

    def taps(r, cb):
        cols = slice(cb, cb + LANES)
        acc_ref[r:r + CONV_ROWS, cols] = (
            _shifted_taps(buf_ref, dww_ref, offsets, r, CONV_ROWS, cols) + dwb_ref[:, cols])

    def finish():
        y = acc_ref[...]
        mu = jnp.mean(y, axis=-1, keepdims=True)
        yc = y - mu
        yn = yc * lax.rsqrt(jnp.mean(yc * yc, axis=-1, keepdims=True) + EPS)
        yn = _silu(yn * lng_ref[...] + lnb_ref[...])
        return _dot(yn.astype(BF16), w_ref[...])

    units = [(lambda r=r, cb=cb: taps(r, cb)) for r in range(0, tl, CONV_ROWS) for cb in range(0, c, LANES)]
    return units, finish


def _ssd_part(z_ref, xbc_ref, dt_ref, cw_ref, cb_ref, dtb_ref, a_ref, dexp_ref, ng_ref, w_ref,
              e_ref, tril_ref, xb_ref, xs_ref, bm_ref, cm_ref, y_ref, st_ref):
    tl = z_ref.shape[0]
    q = SSD_CHUNK
    gw = SSD_GROUP_W

    @pl.when(pl.program_id(1) == 0)
    def _():
        st_ref[...] = jnp.zeros_like(st_ref)
        xb_ref[0:SSD_TAIL, :] = jnp.zeros((SSD_TAIL, SSD_XBC), F32)

    xb_ref[SSD_TAIL:SSD_TAIL + tl, :] = xbc_ref[...].astype(F32)
    first = SSD_TAIL - (SSD_CONV - 1)
    offsets = [first + j for j in range(SSD_CONV)]
    for r in range(0, tl, SSD_CONV_ROWS):
        rows = slice(r, r + SSD_CONV_ROWS)
        for cbk in range(0, SSD_XBC, SSD_CONV_COLS):
            cols = slice(cbk, cbk + SSD_CONV_COLS)
            act = _silu(_shifted_taps(xb_ref, cw_ref, offsets, r, SSD_CONV_ROWS, cols) + cb_ref[:, cols])
            if cbk < SSD_INNER:
                xs_ref[rows, cols] = act
            elif cbk < SSD_INNER + SSD_BC:
                bm_ref[rows, cbk - SSD_INNER:cbk - SSD_INNER + SSD_CONV_COLS] = act
            else:
                off = cbk - SSD_INNER - SSD_BC
                cm_ref[rows, off:off + SSD_CONV_COLS] = act
    xb_ref[0:SSD_TAIL, :] = xb_ref[tl:tl + SSD_TAIL, :]

    x = dt_ref[...] + dtb_ref[...]
    dt_all = jnp.maximum(x, 0.0) + jnp.log1p(jnp.exp(-jnp.abs(x)))

    row = lax.broadcasted_iota(jnp.int32, (q, q), 0)
    col = lax.broadcasted_iota(jnp.int32, (q, q), 1)
    tri = row >= col
    lane = lax.broadcasted_iota(jnp.int32, (q, LANES), 1)
    lo_half = lane < SSD_HEAD_DIM
    tril = tril_ref[...]

    ctx = {}

    def chunk_head(r0):
        rows = slice(r0, r0 + q)
        dt_c = dt_all[rows]
        da = dt_c * a_ref[...]
        hi, mid, lo = _split3(da)
        cs = _dot(tril, hi) + _dot(tril, mid) + _dot(tril, lo)
        cs_t = cs.T
        dt_t = dt_c.T
        cs_last = cs[q - 1:q, :]
        w_dec = jnp.exp(cs_last - cs) * dt_c
        dfs = jnp.exp(cs)
        cd = jnp.exp(cs_last)
        cd_hi = cd.astype(BF16)
        cd_lo = (cd - cd_hi.astype(F32)).astype(BF16)
        stack = jnp.concatenate([
            w_dec.astype(BF16), dfs.astype(BF16),
            jnp.broadcast_to(cd_hi, (SUBLANES * 2, LANES)),
            jnp.broadcast_to(cd_lo, (SUBLANES * 2, LANES))], axis=0)
        ex = _dot(stack, e_ref[...])
        w_exp = ex[0:q]
        dfs_exp = ex[q:2 * q]
        cd_exp = ex[2 * q:2 * q + 1] + ex[2 * q + 2 * SUBLANES:2 * q + 2 * SUBLANES + 1]

        ctx[r0] = (cs, cs_t, dt_t, w_exp, dfs_exp, cd_exp)

    def chunk_group(r0, g):
        rows = slice(r0, r0 + q)
        cs, cs_t, dt_t, w_exp, dfs_exp, cd_exp = ctx[r0]
        lanes_g = slice(g * gw, (g + 1) * gw)
        xs_g = xs_ref[rows, lanes_g]
        b_g = bm_ref[rows, g * SSD_STATE:(g + 1) * SSD_STATE]
        c_g = cm_ref[rows, g * SSD_STATE:(g + 1) * SSD_STATE].astype(BF16)
        cbm = _dot_nt(c_g, b_g.astype(BF16))
        st_g = st_ref[:, lanes_g]
        y_off = _dot(c_g, st_g.astype(BF16)) * dfs_exp[:, lanes_g]
        for pr in range(gw // LANES):
            h0 = (g * gw + pr * LANES) // SSD_HEAD_DIM
            sc = []
            for h in (h0, h0 + 1):
                diff = cs[:, h:h + 1] - cs_t[h:h + 1, :]
                dec = jnp.where(tri, jnp.exp(diff), 0.0)
                sc.append((cbm * dec * dt_t[h:h + 1, :]).astype(BF16))
            lhs = jnp.concatenate(sc, axis=1)
            lanes_p = slice(pr * LANES, (pr + 1) * LANES)
            xp = xs_g[:, lanes_p].astype(BF16)
            zero = jnp.zeros_like(xp)
            rhs = jnp.concatenate([jnp.where(lo_half, xp, zero),
                                   jnp.where(lo_half, zero, xp)], axis=0)
            y_ref[rows, g * gw + pr * LANES:g * gw + (pr + 1) * LANES] = y_off[:, lanes_p] + _dot(lhs, rhs)
        xw = (xs_g * w_exp[:, lanes_g]).astype(BF16)
        st_ref[:, lanes_g] = st_g * cd_exp[:, lanes_g] + _dot(b_g.T.astype(BF16), xw)

    def finish():
        y = y_ref[...] + xs_ref[...] * dexp_ref[...]
        y = y * _silu(z_ref[...].astype(F32))
        parts = []
        for g in range(SSD_GROUPS):
            yg = y[:, g * gw:(g + 1) * gw]
            parts.append(yg * lax.rsqrt(jnp.mean(yg * yg, axis=-1, keepdims=True) + EPS))
        yn = jnp.concatenate(parts, axis=1) * ng_ref[...]
        return _dot(yn.astype(BF16), w_ref[...])

    for r0 in range(0, tl, q):
        chunk_head(r0)
    units = [(lambda r0=r0, g=g: chunk_group(r0, g)) for r0 in range(0, tl, q) for g in range(SSD_GROUPS)]
    return units, finish


def _swa_part(sink_ref, q_ref, kc_ref, vc_ref, kp_ref, vp_ref, bias_ref, qg_ref, kg_ref, ek_ref,
              rep_ref, w_ref, att_ref):
    tq = q_ref.shape[0]
    blk = ATTN_BLOCK
    gwid = ATTN_GROUP_W
    inv_d = 1.0 / ATTN_HEAD_DIM

    qf = q_ref[...].astype(F32)
    q_sq = (qf * qf).astype(BF16)
    q_ss = jnp.concatenate([_dot(q_sq[:, g * gwid:(g + 1) * gwid], ek_ref[...])
                            for g in range(ATTN_KV_HEADS)], axis=1)
    qn = (qf * lax.rsqrt(q_ss * inv_d + EPS) * qg_ref[...] * (ATTN_HEAD_DIM ** -0.5)).astype(BF16)

    kf = jnp.concatenate([kp_ref[...], kc_ref[...]], axis=0).astype(F32)
    k_ss = _dot((kf * kf).astype(BF16), ek_ref[...])
    kn = (kf * lax.rsqrt(k_ss * inv_d + EPS) * kg_ref[...]).astype(BF16)
    k_rep = _dot(kn, rep_ref[...]).astype(BF16)
    v_all = jnp.concatenate([vp_ref[...], vc_ref[...]], axis=0)
    v_rep = _dot(v_all, rep_ref[...]).astype(BF16)

    lane = lax.broadcasted_iota(jnp.int32, (1, gwid), 1) // ATTN_HEAD_DIM
    key_col = lax.broadcasted_iota(jnp.int32, (blk, 2 * blk), 1)
    first_tile = pl.program_id(1) == 0

    def block_group(n, g):
        r0 = n * blk
        lanes_g = slice(g * gwid, (g + 1) * gwid)
        q_g = qn[r0:r0 + blk, lanes_g]
        k_g = k_rep[r0:r0 + 2 * blk, lanes_g]
        v_g = v_rep[r0:r0 + 2 * blk, lanes_g]
        probs = []
        v_blocks = []
        for hh in range(ATTN_REP):
            h = g * ATTN_REP + hh
            head = lane == hh
            s = _dot_nt(jnp.where(head, q_g, jnp.zeros_like(q_g)), k_g) + bias_ref[h]
            if n == 0:
                s = jnp.where(jnp.logical_and(first_tile, key_col < blk), NEG_INF, s)
            sink = sink_ref[h]
            m = jnp.maximum(jnp.max(s, axis=-1, keepdims=True), sink)
            pe = jnp.exp(s - m)
            den = jnp.sum(pe, axis=-1, keepdims=True) + jnp.exp(sink - m)
            probs.append((pe * (1.0 / den)).astype(BF16))
            v_blocks.append(jnp.where(head, v_g, jnp.zeros_like(v_g)))
        att_ref[r0:r0 + blk, lanes_g] = _dot(jnp.concatenate(probs, axis=1),
                                             jnp.concatenate(v_blocks, axis=0))

    def finish():
        return _dot(att_ref[...].astype(BF16), w_ref[...])

    units = [(lambda n=n, g=g: block_group(n, g)) for n in range(tq // blk) for g in range(ATTN_KV_HEADS)]
    return units, finish


def _mixers_kernel(sink_ref, gates_ref, gb_ref,
                   cur_ref, halo_ref, dww_ref, dwb_ref, lng_ref, lnb_ref, wa_ref,
                   z_ref, xbc_ref, dt_ref, cw_ref, cb_ref, dtb_ref, a_ref, dexp_ref, ng_ref, wb_ref,
                   e_ref, tril_ref,
                   q_ref, kc_ref, vc_ref, kp_ref, vp_ref, bias_ref, qg_ref, kg_ref, ek_ref, rep_ref,
                   wc_ref,
                   o_ref,
                   cbuf_ref, cacc_ref, xb_ref, xs_ref, bm_ref, cm_ref, y_ref, st_ref, att_ref):
    conv_units, conv_finish = _conv_part(cur_ref, halo_ref, dww_ref, dwb_ref, lng_ref, lnb_ref, wa_ref,
                                         cbuf_ref, cacc_ref)
    ssd_units, ssd_finish = _ssd_part(z_ref, xbc_ref, dt_ref, cw_ref, cb_ref, dtb_ref, a_ref, dexp_ref,
                                      ng_ref, wb_ref, e_ref, tril_ref, xb_ref, xs_ref, bm_ref, cm_ref,
                                      y_ref, st_ref)
    swa_units, swa_finish = _swa_part(sink_ref, q_ref, kc_ref, vc_ref, kp_ref, vp_ref, bias_ref, qg_ref,
                                      kg_ref, ek_ref, rep_ref, wc_ref, att_ref)
    for unit in _round_robin(ssd_units, swa_units, conv_units):
        unit()
    merged = (_gate(gates_ref, gb_ref, 0) * conv_finish() + _gate(gates_ref, gb_ref, 1) * ssd_finish()
              + _gate(gates_ref, gb_ref, 2) * swa_finish())
    o_ref[...] = merged.astype(BF16)


def _mixers(p, dt, sinks, gb, conv_consts, ssd_consts, swa_consts, batch, seq, tl):
    nl = seq // tl
    cw = 2 * CONV_DIM
    kvw = ATTN_KV_DIM
    gw = N_BRANCH * D_MODEL
    halo_per_tile = tl // CONV_HALO
    blk_per_tile = tl // ATTN_BLOCK

    def tile(width, col_off):
        return pl.BlockSpec((tl, width), lambda b, l: (b * nl + l, col_off // width))

    def prev_rows(rows, per_tile, width, col_off):
        return pl.BlockSpec(
            (rows, width), lambda b, l: (jnp.maximum((b * nl + l) * per_tile - 1, 0), col_off // width))

    in_specs = (
        [pl.BlockSpec(memory_space=pltpu.SMEM), tile(gw, P_GATE), _const_spec(gb.shape)]
        + [tile(cw, P_CONV), prev_rows(CONV_HALO, halo_per_tile, cw, P_CONV)]
        + [_const_spec(a.shape) for a in conv_consts]
        + [tile(SSD_INNER, P_Z), tile(SSD_XBC, P_XBC),
           pl.BlockSpec((tl, DT_PAD), lambda b, l: (b * nl + l, 0))]
        + [_const_spec(a.shape) for a in ssd_consts]
        + [tile(ATTN_DIM, P_Q), tile(kvw, P_K), tile(kvw, P_V),
           prev_rows(ATTN_BLOCK, blk_per_tile, kvw, P_K), prev_rows(ATTN_BLOCK, blk_per_tile, kvw, P_V)]
        + [_const_spec(a.shape) for a in swa_consts])
    return pl.pallas_call(
        _mixers_kernel,
        grid=(batch, nl),
        in_specs=in_specs,
        out_specs=pl.BlockSpec((tl, D_MODEL), lambda b, l: (b * nl + l, 0)),
        out_shape=jax.ShapeDtypeStruct((batch * seq, D_MODEL), BF16),
        scratch_shapes=[
            pltpu.VMEM((CONV_HALO + tl, CONV_DIM), F32),
            pltpu.VMEM((tl, CONV_DIM), F32),
            pltpu.VMEM((SSD_TAIL + tl, SSD_XBC), F32),
            pltpu.VMEM((tl, SSD_INNER), F32),
            pltpu.VMEM((tl, SSD_BC), F32),
            pltpu.VMEM((tl, SSD_BC), F32),
            pltpu.VMEM((tl, SSD_INNER), F32),
            pltpu.VMEM((SSD_STATE, SSD_INNER), F32),
            pltpu.VMEM((tl, ATTN_DIM), F32),
        ],
        compiler_params=_cparams(("parallel", "arbitrary")),
        name="mixers",
    )(sinks, p, gb, p, p, *conv_consts, p, p, dt, *ssd_consts, p, p, p, p, p, *swa_consts)


def _band_buckets():
    qi = np.arange(ATTN_BLOCK)[:, None] + ATTN_BLOCK
    kj = np.arange(2 * ATTN_BLOCK)[None, :]
    dist = qi - kj
    max_exact = REL_BUCKETS // 2
    d = np.maximum(dist, 1).astype(np.float32)
    large = max_exact + (np.log(d / np.float32(max_exact)) / np.float32(math.log(REL_MAX_DIST / max_exact))
                         * np.float32(REL_BUCKETS - max_exact)).astype(np.int32)
    large = np.minimum(large, REL_BUCKETS - 1)
    bucket = np.where(dist < max_exact, np.maximum(dist, 0), large).astype(np.int32)
    in_window = (dist >= 0) & (dist < ATTN_WINDOW)
    return bucket, in_window.astype(np.int32)


def _bias_kernel(tab_ref, bucket_ref, win_ref, o_ref):
    bucket = bucket_ref[...]
    win = win_ref[...] > 0
    for h in range(ATTN_HEADS):
        acc = jnp.zeros(bucket.shape, F32)
        for b in range(REL_BUCKETS):
            acc = jnp.where(bucket == b, tab_ref[b, h], acc)
        o_ref[h] = jnp.where(win, acc, NEG_INF)


def _band_bias(rel_table):
    bucket, win = _band_buckets()
    shape = (ATTN_BLOCK, 2 * ATTN_BLOCK)
    return pl.pallas_call(
        _bias_kernel,
        in_specs=[pl.BlockSpec(memory_space=pltpu.SMEM),
                  pl.BlockSpec(shape, lambda: (0, 0)),
                  pl.BlockSpec(shape, lambda: (0, 0))],
        out_specs=pl.BlockSpec((ATTN_HEADS,) + shape, lambda: (0, 0, 0)),
        out_shape=jax.ShapeDtypeStruct((ATTN_HEADS,) + shape, F32),
        name="band_bias",
    )(rel_table, jnp.asarray(bucket), jnp.asarray(win))


def _memkv_kernel(mem_ref, g_ref, w_ref, kg_ref, k_ref, v_ref):
    mn = _rms(mem_ref[0], g_ref[0]).astype(BF16)
    kv = _dot(mn, w_ref[0])
    parts = []
    for h in range(XATTN_HEADS):
        kh = kv[:, h * XATTN_HEAD_DIM:(h + 1) * XATTN_HEAD_DIM]
        parts.append(_rms(kh, kg_ref[0]))
    k_ref[0, 0] = jnp.concatenate(parts, axis=1).astype(BF16)
    v_ref[0, 0] = kv[:, D_MODEL:].astype(BF16)


def _mem_kv(mem, g, w, kg):
    depth = w.shape[0]
    batch = mem.shape[0]
    out = jax.ShapeDtypeStruct((depth, batch, MEM_LEN, D_MODEL), BF16)
    return pl.pallas_call(
        _memkv_kernel,
        grid=(depth, batch),
        in_specs=[
            pl.BlockSpec((1, MEM_LEN, D_MODEL), lambda i, b: (b, 0, 0)),
            pl.BlockSpec((1, 1, D_MODEL), lambda i, b: (i, 0, 0)),
            pl.BlockSpec((1, D_MODEL, 2 * D_MODEL), lambda i, b: (i, 0, 0)),
            pl.BlockSpec((1, 1, XATTN_HEAD_DIM), lambda i, b: (i, 0, 0)),
        ],
        out_specs=[pl.BlockSpec((1, 1, MEM_LEN, D_MODEL), lambda i, b: (i, b, 0, 0)),
                   pl.BlockSpec((1, 1, MEM_LEN, D_MODEL), lambda i, b: (i, b, 0, 0))],
        out_shape=[out, out],
        compiler_params=_cparams(("arbitrary", "arbitrary")),
        name="mem_kv",
    )(mem, g, w, kg)


def _tail_body(h_ref, m_ref, k_ref, v_ref, wmix_ref, gx_ref, wq_ref, qg_ref, wo_ref, gm_ref,
               wup_ref, wdn_ref, att_ref):
    h = h_ref[...] + _dot(m_ref[...], wmix_ref[...])

    qx = _dot(_rms(h, gx_ref[...]).astype(BF16), wq_ref[...])
    for hd in range(XATTN_HEADS):
        lanes_h = slice(hd * XATTN_HEAD_DIM, (hd + 1) * XATTN_HEAD_DIM)
        qh = (_rms(qx[:, lanes_h], qg_ref[...]) * (XATTN_HEAD_DIM ** -0.5)).astype(BF16)
        s = _dot_nt(qh, k_ref[0, 0, :, lanes_h])
        m = jnp.max(s, axis=-1, keepdims=True)
        pe = jnp.exp(s - m)
        pr = pe * (1.0 / jnp.sum(pe, axis=-1, keepdims=True))
        att_ref[:, lanes_h] = _dot(pr.astype(BF16), v_ref[0, 0, :, lanes_h])
    h = h + _dot(att_ref[...].astype(BF16), wo_ref[...])

    u = _rms(h, gm_ref[...]).astype(BF16)
    acc = h
    for cb in range(0, MLP_HIDDEN, D_MODEL):
        up = jnp.maximum(_dot(u, wup_ref[:, cb:cb + D_MODEL]), 0.0)
        acc = acc + _dot((up * up).astype(BF16), wdn_ref[cb:cb + D_MODEL, :])
    return acc


def _tail_kernel(h_ref, m_ref, k_ref, v_ref, wmix_ref, gx_ref, wq_ref, qg_ref, wo_ref, gm_ref,
                 wup_ref, wdn_ref, gn_ref, wdt_ref, o_ref, u_ref, dt_ref, att_ref):
    h = _tail_body(h_ref, m_ref, k_ref, v_ref, wmix_ref, gx_ref, wq_ref, qg_ref, wo_ref, gm_ref,
                   wup_ref, wdn_ref, att_ref)
    o_ref[...] = h
    u = _rms(h, gn_ref[...]).astype(BF16)
    u_ref[...] = u
    dt_ref[...] = _dot(u, wdt_ref[...])


def _tail_last_kernel(h_ref, m_ref, k_ref, v_ref, wmix_ref, gx_ref, wq_ref, qg_ref, wo_ref, gm_ref,
                      wup_ref, wdn_ref, o_ref, att_ref):
    o_ref[...] = _tail_body(h_ref, m_ref, k_ref, v_ref, wmix_ref, gx_ref, wq_ref, qg_ref, wo_ref,
                            gm_ref, wup_ref, wdn_ref, att_ref)


def _tail(h2d, merged, k_mem, v_mem, layer, consts, next_consts, batch, seq, tl):
    nl = seq // tl
    t = batch * seq
    act = lambda b, l: (b * nl + l, 0)
    kv_spec = pl.BlockSpec((1, 1, MEM_LEN, D_MODEL), lambda b, l: (layer, b, 0, 0))
    in_specs = ([pl.BlockSpec((tl, D_MODEL), act), pl.BlockSpec((tl, D_MODEL), act), kv_spec, kv_spec]
                + [_const_spec(a.shape) for a in consts])
    out_specs = [pl.BlockSpec((tl, D_MODEL), act)]
    out_shape = [jax.ShapeDtypeStruct((t, D_MODEL), F32)]
    if next_consts is None:
        body, extra = _tail_last_kernel, ()
    else:
        body, extra = _tail_kernel, tuple(next_consts)
        in_specs += [_const_spec(a.shape) for a in extra]
        out_specs += [pl.BlockSpec((tl, D_MODEL), act), pl.BlockSpec((tl, DT_PAD), act)]
        out_shape += [jax.ShapeDtypeStruct((t, D_MODEL), BF16), jax.ShapeDtypeStruct((t, DT_PAD), F32)]
    return pl.pallas_call(
        body,
        grid=(batch, nl),
        in_specs=in_specs,
        out_specs=out_specs,
        out_shape=out_shape,
        scratch_shapes=[pltpu.VMEM((tl, D_MODEL), F32)],
        compiler_params=_cparams(("parallel", "arbitrary")),
        name="tail",
    )(h2d, merged, k_mem, v_mem, *consts, *extra)


def _head_indicator(width, head_dim):
    idx = np.arange(width) // head_dim
    return (idx[:, None] == idx[None, :]).astype(np.float32)


def _tile_sizes(seq):
    pick = lambda cands: next(c for c in cands if seq % c == 0)
    return dict(tm=pick((1024, 512, 256, 128)), tl_mix=pick((256, 128)), tl_tail=pick((512, 256, 128)))


def kernel(x, mem, rel_table, norm_mix, w_in, gate_bias, conv_dw_w, conv_dw_b, conv_ln_g, conv_ln_b, w_conv_out, ssd_conv_w, ssd_conv_b, ssd_dt_bias, ssd_A_log, ssd_D, ssd_norm_g, w_ssd_out, attn_q_norm, attn_k_norm, attn_sinks, w_attn_out, w_mix_out, norm_xattn, norm_mem, w_xq, w_xkv, xattn_q_norm, xattn_k_norm, w_xo, norm_mlp, w_mlp_up, w_mlp_down):
    batch, seq, _ = x.shape
    depth = w_in.shape[0]
    ts = _tile_sizes(seq)
    assert seq % ATTN_BLOCK == 0 and seq % SSD_CHUNK == 0

    w_p = jnp.concatenate([
        w_in[:, :, OFF_GATE:IN_COLS], w_in[:, :, OFF_XBC:OFF_DT], w_in[:, :, OFF_CONV:OFF_Z],
        w_in[:, :, OFF_Z:OFF_XBC], w_in[:, :, OFF_Q:OFF_GATE]], axis=2).astype(BF16)
    w_dt = jnp.pad(w_in[:, :, OFF_DT:OFF_Q], ((0, 0), (0, 0), (0, DT_PAD - SSD_HEADS))).astype(BF16)
    pad_h = ((0, 0), (0, DT_PAD - SSD_HEADS))
    dt_bias = jnp.pad(ssd_dt_bias, pad_h)[:, None, :]
    a_row = -jnp.exp(jnp.pad(ssd_A_log, pad_h))[:, None, :]
    d_exp = jnp.repeat(ssd_D, SSD_HEAD_DIM, axis=1)[:, None, :]

    e_np = np.zeros((DT_PAD, SSD_INNER), np.float32)
    e_np[np.arange(SSD_INNER) // SSD_HEAD_DIM, np.arange(SSD_INNER)] = 1.0
    e_mat = jnp.asarray(e_np, BF16)
    tril = jnp.asarray(np.tril(np.ones((SSD_CHUNK, SSD_CHUNK), np.float32)), BF16)
    ek = jnp.asarray(_head_indicator(ATTN_KV_DIM, ATTN_HEAD_DIM), BF16)
    rep_np = np.zeros((ATTN_KV_DIM, ATTN_DIM), np.float32)
    lanes = np.arange(ATTN_DIM)
    rep_np[(lanes // ATTN_GROUP_W) * ATTN_HEAD_DIM + lanes % ATTN_HEAD_DIM, lanes] = 1.0
    rep = jnp.asarray(rep_np, BF16)

    bias = _band_bias(rel_table)
    k_mem, v_mem = _mem_kv(mem, norm_mem[:, None, :], w_xkv.astype(BF16), xattn_k_norm[:, None, :])

    row = lambda a, i: a[i][None, :]
    h = x.reshape(batch * seq, D_MODEL)
    u, dt = _norm_dt(h, row(norm_mix, 0), w_dt[0], ts["tm"])
    for i in range(depth):
        p = _inproj(u, w_p[i], ts["tm"], P_COLS // 4)
        conv_consts = (conv_dw_w[i], row(conv_dw_b, i), row(conv_ln_g, i), row(conv_ln_b, i),
                       w_conv_out[i].astype(BF16))
        ssd_consts = (ssd_conv_w[i], row(ssd_conv_b, i), dt_bias[i], a_row[i], d_exp[i],
                      row(ssd_norm_g, i), w_ssd_out[i].astype(BF16), e_mat, tril)
        swa_consts = (bias, jnp.tile(attn_q_norm[i], ATTN_HEADS)[None, :],
                      jnp.tile(attn_k_norm[i], ATTN_KV_HEADS)[None, :], ek, rep,
                      w_attn_out[i].astype(BF16))
        merged = _mixers(p, dt, attn_sinks[i], gate_bias[i], conv_consts, ssd_consts, swa_consts,
                         batch, seq, ts["tl_mix"])
        tail_consts = (w_mix_out[i].astype(BF16), row(norm_xattn, i), w_xq[i].astype(BF16),
                       row(xattn_q_norm, i), w_xo[i].astype(BF16), row(norm_mlp, i),
                       w_mlp_up[i].astype(BF16), w_mlp_down[i].astype(BF16))
        if i + 1 < depth:
            h, u, dt = _tail(h, merged, k_mem, v_mem, i, tail_consts, (row(norm_mix, i + 1), w_dt[i + 1]),
                             batch, seq, ts["tl_tail"])
        else:
            (h,) = _tail(h, merged, k_mem, v_mem, i, tail_consts, None, batch, seq, ts["tl_tail"])
    return h.reshape(batch, seq, D_MODEL)
```

```python
import math

import numpy as np
import jax
import jax.numpy as jnp
from jax import lax
from jax.experimental import pallas as pl
from jax.experimental.pallas import tpu as pltpu

F32 = jnp.float32
BF16 = jnp.bfloat16

D_MODEL = 1024
MEM_LEN = 256
EPS = 1e-6
NEG_INF = -1e30

CONV_DIM = D_MODEL
CONV_KERNEL = 31

SSD_INNER = 2 * D_MODEL
SSD_HEAD_DIM = 64
SSD_HEADS = SSD_INNER // SSD_HEAD_DIM
SSD_GROUPS = 4
SSD_STATE = 128
SSD_CONV = 4
SSD_CHUNK = 128
SSD_BC = SSD_GROUPS * SSD_STATE
SSD_XBC = SSD_INNER + 2 * SSD_BC
SSD_GROUP_W = SSD_INNER // SSD_GROUPS

ATTN_HEADS = 16
ATTN_KV_HEADS = 4
ATTN_HEAD_DIM = 64
ATTN_DIM = ATTN_HEADS * ATTN_HEAD_DIM
ATTN_KV_DIM = ATTN_KV_HEADS * ATTN_HEAD_DIM
ATTN_REP = ATTN_HEADS // ATTN_KV_HEADS
ATTN_WINDOW = 128
ATTN_BLOCK = 128
ATTN_GROUP_W = ATTN_REP * ATTN_HEAD_DIM

REL_BUCKETS = 32
REL_MAX_DIST = 128

XATTN_HEADS = 4
XATTN_HEAD_DIM = D_MODEL // XATTN_HEADS

N_BRANCH = 3
MLP_HIDDEN = 4 * D_MODEL

OFF_CONV = 0
OFF_Z = OFF_CONV + 2 * CONV_DIM
OFF_XBC = OFF_Z + SSD_INNER
OFF_DT = OFF_XBC + SSD_XBC
OFF_Q = OFF_DT + SSD_HEADS
OFF_K = OFF_Q + ATTN_DIM
OFF_V = OFF_K + ATTN_KV_DIM
OFF_GATE = OFF_V + ATTN_KV_DIM
IN_COLS = OFF_GATE + N_BRANCH * D_MODEL

LANES = 128
SUBLANES = 8

P_GATE = 0
P_XBC = P_GATE + N_BRANCH * D_MODEL
P_CONV = P_XBC + SSD_XBC
P_Z = P_CONV + 2 * CONV_DIM
P_Q = P_Z + SSD_INNER
P_K = P_Q + ATTN_DIM
P_V = P_K + ATTN_KV_DIM
P_COLS = P_V + ATTN_KV_DIM
DT_PAD = LANES

VMEM_LIMIT = 56 * 1024 * 1024


def _cparams(semantics):
    return pltpu.CompilerParams(dimension_semantics=semantics, vmem_limit_bytes=VMEM_LIMIT)


def _const_spec(shape):
    nd = len(shape)
    return pl.BlockSpec(shape, lambda *_: (0,) * nd, pipeline_mode=pl.Buffered(1))


NEG_LOG2_E = -1.4426950408889634


def _sigmoid(x):
    return 1.0 / (1.0 + jnp.exp2(x * NEG_LOG2_E))


def _silu(x):
    return x * _sigmoid(x)


def _rms(x, g):
    return x * lax.rsqrt(jnp.mean(x * x, axis=-1, keepdims=True) + EPS) * g


def _dot(a, b):
    return jnp.dot(a, b, preferred_element_type=F32)


def _dot_nt(a, b):
    return lax.dot_general(a, b, (((1,), (1,)), ((), ())), preferred_element_type=F32)


def _shifted_taps(buf_ref, w_ref, offsets, r, rows, cols):
    acc = None
    for s in range(SUBLANES):
        group = [(j, o) for j, o in enumerate(offsets) if o % SUBLANES == s]
        if not group:
            continue
        n = rows if s == 0 else rows + SUBLANES
        t_s = None
        for j, o in group:
            base = r + o - s
            term = w_ref[j:j + 1, cols] * buf_ref[base:base + n, cols]
            t_s = term if t_s is None else t_s + term
        part = t_s if s == 0 else _shift_rows(t_s, s, rows)
        acc = part if acc is None else acc + part
    return acc


def _shift_rows(t, s, rows):
    tiles = [pltpu.roll(t[i:i + SUBLANES], SUBLANES - s, axis=0) for i in range(0, rows + SUBLANES, SUBLANES)]
    keep = lax.broadcasted_iota(jnp.int32, tiles[0].shape, 0) < SUBLANES - s
    return jnp.concatenate([jnp.where(keep, a, b) for a, b in zip(tiles[:-1], tiles[1:])], axis=0)


def _split3(x):
    hi = x.astype(BF16)
    r1 = x - hi.astype(F32)
    mid = r1.astype(BF16)
    lo = (r1 - mid.astype(F32)).astype(BF16)
    return hi, mid, lo


def _norm_dt_kernel(x_ref, g_ref, wdt_ref, u_ref, dt_ref):
    u = _rms(x_ref[...], g_ref[...]).astype(BF16)
    u_ref[...] = u
    dt_ref[...] = _dot(u, wdt_ref[...])


def _norm_dt(h2d, g, wdt, tm):
    t = h2d.shape[0]
    return pl.pallas_call(
        _norm_dt_kernel,
        grid=(t // tm,),
        in_specs=[pl.BlockSpec((tm, D_MODEL), lambda i: (i, 0)),
                  _const_spec((1, D_MODEL)),
                  _const_spec((D_MODEL, DT_PAD))],
        out_specs=[pl.BlockSpec((tm, D_MODEL), lambda i: (i, 0)),
                   pl.BlockSpec((tm, DT_PAD), lambda i: (i, 0))],
        out_shape=[jax.ShapeDtypeStruct((t, D_MODEL), BF16),
                   jax.ShapeDtypeStruct((t, DT_PAD), F32)],
        compiler_params=_cparams(("parallel",)),
        name="norm_dt",
    )(h2d, g, wdt)


def _inproj_kernel(u_ref, w_ref, p_ref):
    p_ref[...] = _dot(u_ref[...], w_ref[...]).astype(BF16)


def _inproj(u, w, tm, tn):
    t = u.shape[0]
    return pl.pallas_call(
        _inproj_kernel,
        grid=(t // tm, P_COLS // tn),
        in_specs=[pl.BlockSpec((tm, D_MODEL), lambda i, j: (i, 0)),
                  pl.BlockSpec((D_MODEL, tn), lambda i, j: (0, j))],
        out_specs=pl.BlockSpec((tm, tn), lambda i, j: (i, j)),
        out_shape=jax.ShapeDtypeStruct((t, P_COLS), BF16),
        compiler_params=_cparams(("parallel", "arbitrary")),
        name="inproj",
    )(u, w)


CONV_HALO = 32
CONV_ROWS = 128
SSD_TAIL = SUBLANES
SSD_CONV_ROWS = 64
SSD_CONV_COLS = 256


def _round_robin(*streams):
    total = max(len(s) for s in streams)
    order = []
    for step in range(total):
        for s in streams:
            lo = step * len(s) // total
            hi = (step + 1) * len(s) // total
            order.extend(s[lo:hi])
    return order


def _gate(gates_ref, gb_ref, branch):
    cols = slice(branch * D_MODEL, (branch + 1) * D_MODEL)
    return _sigmoid(gates_ref[:, cols].astype(F32) + gb_ref[branch:branch + 1, :])


def _conv_part(cur_ref, halo_ref, dww_ref, dwb_ref, lng_ref, lnb_ref, w_ref, buf_ref, acc_ref):
    tl = cur_ref.shape[0]
    c = CONV_DIM
    cur = cur_ref[...].astype(F32)
    buf_ref[CONV_HALO:CONV_HALO + tl, :] = cur[:, :c] * _sigmoid(cur[:, c:])
    hal = halo_ref[...].astype(F32)
    hglu = hal[:, :c] * _sigmoid(hal[:, c:])
    buf_ref[0:CONV_HALO, :] = jnp.where(pl.program_id(1) > 0, hglu, 0.0)

    first = CONV_HALO - (CONV_KERNEL - 1)
    offsets = [first + j for j in range(CONV_KERNEL)]

    def taps(r, cb):
        cols = slice(cb, cb + LANES)
        acc_ref[r:r + CONV_ROWS, cols] = (
            _shifted_taps(buf_ref, dww_ref, offsets, r, CONV_ROWS, cols) + dwb_ref[:, cols])

    def finish():
        y = acc_ref[...]
        mu = jnp.mean(y, axis=-1, keepdims=True)
        yc = y - mu
        yn = yc * lax.rsqrt(jnp.mean(yc * yc, axis=-1, keepdims=True) + EPS)
        yn = _silu(yn * lng_ref[...] + lnb_ref[...])
        return _dot(yn.astype(BF16), w_ref[...])

    units = [(lambda r=r, cb=cb: taps(r, cb)) for r in range(0, tl, CONV_ROWS) for cb in range(0, c, LANES)]
    return units, finish


def _ssd_part(z_ref, xbc_ref, dt_ref, cw_ref, cb_ref, dtb_ref, a_ref, dexp_ref, ng_ref, w_ref,
              e_ref, tril_ref, xb_ref, xs_ref, bm_ref, cm_ref, y_ref, st_ref):
    tl = z_ref.shape[0]
    q = SSD_CHUNK
    gw = SSD_GROUP_W

    @pl.when(pl.program_id(1) == 0)
    def _():
        st_ref[...] = jnp.zeros_like(st_ref)
        xb_ref[0:SSD_TAIL, :] = jnp.zeros((SSD_TAIL, SSD_XBC), F32)

    xb_ref[SSD_TAIL:SSD_TAIL + tl, :] = xbc_ref[...].astype(F32)
    first = SSD_TAIL - (SSD_CONV - 1)
    offsets = [first + j for j in range(SSD_CONV)]
    for r in range(0, tl, SSD_CONV_ROWS):
        rows = slice(r, r + SSD_CONV_ROWS)
        for cbk in range(0, SSD_XBC, SSD_CONV_COLS):
            cols = slice(cbk, cbk + SSD_CONV_COLS)
            act = _silu(_shifted_taps(xb_ref, cw_ref, offsets, r, SSD_CONV_ROWS, cols) + cb_ref[:, cols])
            if cbk < SSD_INNER:
                xs_ref[rows, cols] = act
            elif cbk < SSD_INNER + SSD_BC:
                bm_ref[rows, cbk - SSD_INNER:cbk - SSD_INNER + SSD_CONV_COLS] = act
            else:
                off = cbk - SSD_INNER - SSD_BC
                cm_ref[rows, off:off + SSD_CONV_COLS] = act
    xb_ref[0:SSD_TAIL, :] = xb_ref[tl:tl + SSD_TAIL, :]

    x = dt_ref[...] + dtb_ref[...]
    dt_all = jnp.maximum(x, 0.0) + jnp.log1p(jnp.exp(-jnp.abs(x)))

    row = lax.broadcasted_iota(jnp.int32, (q, q), 0)
    col = lax.broadcasted_iota(jnp.int32, (q, q), 1)
    tri = row >= col
    lane = lax.broadcasted_iota(jnp.int32, (q, LANES), 1)
    lo_half = lane < SSD_HEAD_DIM
    tril = tril_ref[...]

    ctx = {}

    def chunk_head(r0):
        rows = slice(r0, r0 + q)
        dt_c = dt_all[rows]
        da = dt_c * a_ref[...]
        hi, mid, lo = _split3(da)
        cs = _dot(tril, hi) + _dot(tril, mid) + _dot(tril, lo)
        cs_t = cs.T
        dt_t = dt_c.T
        cs_last = cs[q - 1:q, :]
        w_dec = jnp.exp(cs_last - cs) * dt_c
        dfs = jnp.exp(cs)
        cd = jnp.exp(cs_last)
        cd_hi = cd.astype(BF16)
        cd_lo = (cd - cd_hi.astype(F32)).astype(BF16)
        stack = jnp.concatenate([
            w_dec.astype(BF16), dfs.astype(BF16),
            jnp.broadcast_to(cd_hi, (SUBLANES * 2, LANES)),
            jnp.broadcast_to(cd_lo, (SUBLANES * 2, LANES))], axis=0)
        ex = _dot(stack, e_ref[...])
        w_exp = ex[0:q]
        dfs_exp = ex[q:2 * q]
        cd_exp = ex[2 * q:2 * q + 1] + ex[2 * q + 2 * SUBLANES:2 * q + 2 * SUBLANES + 1]

        ctx[r0] = (cs, cs_t, dt_t, w_exp, dfs_exp, cd_exp)

    def chunk_body(r0):
        rows = slice(r0, r0 + q)
        cs, cs_t, dt_t, w_exp, dfs_exp, cd_exp = ctx[r0]
        groups = range(SSD_GROUPS)
        pairs = range(gw // LANES)
        lanes = [slice(g * gw, (g + 1) * gw) for g in groups]
        xs_g = [xs_ref[rows, lanes[g]] for g in groups]
        b_g = [bm_ref[rows, g * SSD_STATE:(g + 1) * SSD_STATE] for g in groups]
        c_g = [cm_ref[rows, g * SSD_STATE:(g + 1) * SSD_STATE].astype(BF16) for g in groups]
        st_g = [st_ref[:, lanes[g]] for g in groups]
        cbm = [_dot_nt(c_g[g], b_g[g].astype(BF16)) for g in groups]
        y_off = [_dot(c_g[g], st_g[g].astype(BF16)) * dfs_exp[:, lanes[g]] for g in groups]
        lhs = {}
        for g in groups:
            for pr in pairs:
                h0 = (g * gw + pr * LANES) // SSD_HEAD_DIM
                sc = []
                for h in (h0, h0 + 1):
                    diff = cs[:, h:h + 1] - cs_t[h:h + 1, :]
                    dec = jnp.where(tri, jnp.exp(diff), 0.0)
                    sc.append((cbm[g] * dec * dt_t[h:h + 1, :]).astype(BF16))
                lhs[g, pr] = jnp.concatenate(sc, axis=1)
        for g in groups:
            for pr in pairs:
                lanes_p = slice(pr * LANES, (pr + 1) * LANES)
                xp = xs_g[g][:, lanes_p].astype(BF16)
                zero = jnp.zeros_like(xp)
                rhs = jnp.concatenate([jnp.where(lo_half, xp, zero),
                                       jnp.where(lo_half, zero, xp)], axis=0)
                y_ref[rows, g * gw + pr * LANES:g * gw + (pr + 1) * LANES] = (
                    y_off[g][:, lanes_p] + _dot(lhs[g, pr], rhs))
        for g in groups:
            xw = (xs_g[g] * w_exp[:, lanes[g]]).astype(BF16)
            st_ref[:, lanes[g]] = st_g[g] * cd_exp[:, lanes[g]] + _dot(b_g[g].T.astype(BF16), xw)

    def finish():
        y = y_ref[...] + xs_ref[...] * dexp_ref[...]
        y = y * _silu(z_ref[...].astype(F32))
        parts = []
        for g in range(SSD_GROUPS):
            yg = y[:, g * gw:(g + 1) * gw]
            parts.append(yg * lax.rsqrt(jnp.mean(yg * yg, axis=-1, keepdims=True) + EPS))
        yn = jnp.concatenate(parts, axis=1) * ng_ref[...]
        return _dot(yn.astype(BF16), w_ref[...])

    for r0 in range(0, tl, q):
        chunk_head(r0)
    units = [(lambda r0=r0: chunk_body(r0)) for r0 in range(0, tl, q)]
    return units, finish


def _swa_part(sink_ref, q_ref, kc_ref, vc_ref, kp_ref, vp_ref, bias_ref, qg_ref, kg_ref, ek_ref,
              rep_ref, w_ref, att_ref):
    tq = q_ref.shape[0]
    blk = ATTN_BLOCK
    gwid = ATTN_GROUP_W
    inv_d = 1.0 / ATTN_HEAD_DIM

    qf = q_ref[...].astype(F32)
    q_sq = (qf * qf).astype(BF16)
    q_ss = jnp.concatenate([_dot(q_sq[:, g * gwid:(g + 1) * gwid], ek_ref[...])
                            for g in range(ATTN_KV_HEADS)], axis=1)
    qn = (qf * lax.rsqrt(q_ss * inv_d + EPS) * qg_ref[...] * (ATTN_HEAD_DIM ** -0.5)).astype(BF16)

    kf = jnp.concatenate([kp_ref[...], kc_ref[...]], axis=0).astype(F32)
    k_ss = _dot((kf * kf).astype(BF16), ek_ref[...])
    kn = (kf * lax.rsqrt(k_ss * inv_d + EPS) * kg_ref[...]).astype(BF16)
    k_rep = _dot(kn, rep_ref[...]).astype(BF16)
    v_all = jnp.concatenate([vp_ref[...], vc_ref[...]], axis=0)
    v_rep = _dot(v_all, rep_ref[...]).astype(BF16)

    lane = lax.broadcasted_iota(jnp.int32, (1, gwid), 1) // ATTN_HEAD_DIM
    key_col = lax.broadcasted_iota(jnp.int32, (blk, 2 * blk), 1)
    first_tile = pl.program_id(1) == 0

    cur_side = (lax.broadcasted_iota(jnp.int32, (blk, blk), 1)
                <= lax.broadcasted_iota(jnp.int32, (blk, blk), 0))

    def block(n):
        r0 = n * blk
        folded = []
        for g in range(ATTN_KV_HEADS):
            lanes_g = slice(g * gwid, (g + 1) * gwid)
            q_g = qn[r0:r0 + blk, lanes_g]
            k_g = k_rep[r0:r0 + 2 * blk, lanes_g]
            for hh in range(ATTN_REP):
                h = g * ATTN_REP + hh
                s = _dot_nt(jnp.where(lane == hh, q_g, jnp.zeros_like(q_g)), k_g) + bias_ref[h]
                if n == 0:
                    s = jnp.where(jnp.logical_and(first_tile, key_col < blk), NEG_INF, s)
                folded.append(jnp.maximum(s[:, :blk], s[:, blk:]))
        probs = []
        for h in range(ATTN_HEADS):
            sink = sink_ref[h]
            m = jnp.maximum(jnp.max(folded[h], axis=-1, keepdims=True), sink)
            pe = jnp.exp(folded[h] - m)
            den = jnp.sum(pe, axis=-1, keepdims=True) + jnp.exp(sink - m)
            pr = pe * (1.0 / den)
            probs.append(jnp.concatenate([jnp.where(cur_side, 0.0, pr), jnp.where(cur_side, pr, 0.0)],
                                         axis=1).astype(BF16))
        for g in range(ATTN_KV_HEADS):
            lanes_g = slice(g * gwid, (g + 1) * gwid)
            v_g = v_rep[r0:r0 + 2 * blk, lanes_g]
            v_blocks = [jnp.where(lane == hh, v_g, jnp.zeros_like(v_g)) for hh in range(ATTN_REP)]
            att_ref[r0:r0 + blk, lanes_g] = _dot(
                jnp.concatenate(probs[g * ATTN_REP:(g + 1) * ATTN_REP], axis=1),
                jnp.concatenate(v_blocks, axis=0))

    def finish():
        return _dot(att_ref[...].astype(BF16), w_ref[...])

    units = [(lambda n=n: block(n)) for n in range(tq // blk)]
    return units, finish


def _mixers_kernel(sink_ref, gates_ref, gb_ref,
                   cur_ref, halo_ref, dww_ref, dwb_ref, lng_ref, lnb_ref, wa_ref,
                   z_ref, xbc_ref, dt_ref, cw_ref, cb_ref, dtb_ref, a_ref, dexp_ref, ng_ref, wb_ref,
                   e_ref, tril_ref,
                   q_ref, kc_ref, vc_ref, kp_ref, vp_ref, bias_ref, qg_ref, kg_ref, ek_ref, rep_ref,
                   wc_ref,
                   o_ref,
                   cbuf_ref, cacc_ref, xb_ref, xs_ref, bm_ref, cm_ref, y_ref, st_ref, att_ref):
    conv_units, conv_finish = _conv_part(cur_ref, halo_ref, dww_ref, dwb_ref, lng_ref, lnb_ref, wa_ref,
                                         cbuf_ref, cacc_ref)
    ssd_units, ssd_finish = _ssd_part(z_ref, xbc_ref, dt_ref, cw_ref, cb_ref, dtb_ref, a_ref, dexp_ref,
                                      ng_ref, wb_ref, e_ref, tril_ref, xb_ref, xs_ref, bm_ref, cm_ref,
                                      y_ref, st_ref)
    swa_units, swa_finish = _swa_part(sink_ref, q_ref, kc_ref, vc_ref, kp_ref, vp_ref, bias_ref, qg_ref,
                                      kg_ref, ek_ref, rep_ref, wc_ref, att_ref)
    for unit in _round_robin(ssd_units, swa_units, conv_units):
        unit()
    merged = (_gate(gates_ref, gb_ref, 0) * conv_finish() + _gate(gates_ref, gb_ref, 1) * ssd_finish()
              + _gate(gates_ref, gb_ref, 2) * swa_finish())
    o_ref[...] = merged.astype(BF16)


def _mixers(p, dt, sinks, gb, conv_consts, ssd_consts, swa_consts, batch, seq, tl):
    nl = seq // tl
    cw = 2 * CONV_DIM
    kvw = ATTN_KV_DIM
    gw = N_BRANCH * D_MODEL
    halo_per_tile = tl // CONV_HALO
    blk_per_tile = tl // ATTN_BLOCK

    def tile(width, col_off):
        return pl.BlockSpec((tl, width), lambda b, l: (b * nl + l, col_off // width))

    def prev_rows(rows, per_tile, width, col_off):
        return pl.BlockSpec(
            (rows, width), lambda b, l: (jnp.maximum((b * nl + l) * per_tile - 1, 0), col_off // width))

    in_specs = (
        [pl.BlockSpec(memory_space=pltpu.SMEM), tile(gw, P_GATE), _const_spec(gb.shape)]
        + [tile(cw, P_CONV), prev_rows(CONV_HALO, halo_per_tile, cw, P_CONV)]
        + [_const_spec(a.shape) for a in conv_consts]
        + [tile(SSD_INNER, P_Z), tile(SSD_XBC, P_XBC),
           pl.BlockSpec((tl, DT_PAD), lambda b, l: (b * nl + l, 0))]
        + [_const_spec(a.shape) for a in ssd_consts]
        + [tile(ATTN_DIM, P_Q), tile(kvw, P_K), tile(kvw, P_V),
           prev_rows(ATTN_BLOCK, blk_per_tile, kvw, P_K), prev_rows(ATTN_BLOCK, blk_per_tile, kvw, P_V)]
        + [_const_spec(a.shape) for a in swa_consts])
    return pl.pallas_call(
        _mixers_kernel,
        grid=(batch, nl),
        in_specs=in_specs,
        out_specs=pl.BlockSpec((tl, D_MODEL), lambda b, l: (b * nl + l, 0)),
        out_shape=jax.ShapeDtypeStruct((batch * seq, D_MODEL), BF16),
        scratch_shapes=[
            pltpu.VMEM((CONV_HALO + tl, CONV_DIM), F32),
            pltpu.VMEM((tl, CONV_DIM), F32),
            pltpu.VMEM((SSD_TAIL + tl, SSD_XBC), F32),
            pltpu.VMEM((tl, SSD_INNER), F32),
            pltpu.VMEM((tl, SSD_BC), F32),
            pltpu.VMEM((tl, SSD_BC), F32),
            pltpu.VMEM((tl, SSD_INNER), F32),
            pltpu.VMEM((SSD_STATE, SSD_INNER), F32),
            pltpu.VMEM((tl, ATTN_DIM), F32),
        ],
        compiler_params=_cparams(("parallel", "arbitrary")),
        name="mixers",
    )(sinks, p, gb, p, p, *conv_consts, p, p, dt, *ssd_consts, p, p, p, p, p, *swa_consts)


def _band_buckets():
    qi = np.arange(ATTN_BLOCK)[:, None] + ATTN_BLOCK
    kj = np.arange(2 * ATTN_BLOCK)[None, :]
    dist = qi - kj
    max_exact = REL_BUCKETS // 2
    d = np.maximum(dist, 1).astype(np.float32)
    large = max_exact + (np.log(d / np.float32(max_exact)) / np.float32(math.log(REL_MAX_DIST / max_exact))
                         * np.float32(REL_BUCKETS - max_exact)).astype(np.int32)
    large = np.minimum(large, REL_BUCKETS - 1)
    bucket = np.where(dist < max_exact, np.maximum(dist, 0), large).astype(np.int32)
    in_window = (dist >= 0) & (dist < ATTN_WINDOW)
    return bucket, in_window.astype(np.int32)


def _bias_kernel(tab_ref, bucket_ref, win_ref, o_ref):
    bucket = bucket_ref[...]
    win = win_ref[...] > 0
    for h in range(ATTN_HEADS):
        acc = jnp.zeros(bucket.shape, F32)
        for b in range(REL_BUCKETS):
            acc = jnp.where(bucket == b, tab_ref[b, h], acc)
        o_ref[h] = jnp.where(win, acc, NEG_INF)


def _band_bias(rel_table):
    bucket, win = _band_buckets()
    shape = (ATTN_BLOCK, 2 * ATTN_BLOCK)
    return pl.pallas_call(
        _bias_kernel,
        in_specs=[pl.BlockSpec(memory_space=pltpu.SMEM),
                  pl.BlockSpec(shape, lambda: (0, 0)),
                  pl.BlockSpec(shape, lambda: (0, 0))],
        out_specs=pl.BlockSpec((ATTN_HEADS,) + shape, lambda: (0, 0, 0)),
        out_shape=jax.ShapeDtypeStruct((ATTN_HEADS,) + shape, F32),
        name="band_bias",
    )(rel_table, jnp.asarray(bucket), jnp.asarray(win))


def _memkv_kernel(mem_ref, g_ref, w_ref, kg_ref, k_ref, v_ref):
    mn = _rms(mem_ref[0], g_ref[0]).astype(BF16)
    kv = _dot(mn, w_ref[0])
    parts = []
    for h in range(XATTN_HEADS):
        kh = kv[:, h * XATTN_HEAD_DIM:(h + 1) * XATTN_HEAD_DIM]
        parts.append(_rms(kh, kg_ref[0]))
    k_ref[0, 0] = jnp.concatenate(parts, axis=1).astype(BF16)
    v_ref[0, 0] = kv[:, D_MODEL:].astype(BF16)


def _mem_kv(mem, g, w, kg):
    depth = w.shape[0]
    batch = mem.shape[0]
    out = jax.ShapeDtypeStruct((depth, batch, MEM_LEN, D_MODEL), BF16)
    return pl.pallas_call(
        _memkv_kernel,
        grid=(depth, batch),
        in_specs=[
            pl.BlockSpec((1, MEM_LEN, D_MODEL), lambda i, b: (b, 0, 0)),
            pl.BlockSpec((1, 1, D_MODEL), lambda i, b: (i, 0, 0)),
            pl.BlockSpec((1, D_MODEL, 2 * D_MODEL), lambda i, b: (i, 0, 0)),
            pl.BlockSpec((1, 1, XATTN_HEAD_DIM), lambda i, b: (i, 0, 0)),
        ],
        out_specs=[pl.BlockSpec((1, 1, MEM_LEN, D_MODEL), lambda i, b: (i, b, 0, 0)),
                   pl.BlockSpec((1, 1, MEM_LEN, D_MODEL), lambda i, b: (i, b, 0, 0))],
        out_shape=[out, out],
        compiler_params=_cparams(("arbitrary", "arbitrary")),
        name="mem_kv",
    )(mem, g, w, kg)


def _tail_body(h_ref, m_ref, k_ref, v_ref, wmix_ref, gx_ref, wq_ref, qg_ref, wo_ref, gm_ref,
               wup_ref, wdn_ref, att_ref):
    h = h_ref[...] + _dot(m_ref[...], wmix_ref[...])

    qx = _dot(_rms(h, gx_ref[...]).astype(BF16), wq_ref[...])
    for hd in range(XATTN_HEADS):
        lanes_h = slice(hd * XATTN_HEAD_DIM, (hd + 1) * XATTN_HEAD_DIM)
        qh = (_rms(qx[:, lanes_h], qg_ref[...]) * (XATTN_HEAD_DIM ** -0.5)).astype(BF16)
        s = _dot_nt(qh, k_ref[0, 0, :, lanes_h])
        m = jnp.max(s, axis=-1, keepdims=True)
        pe = jnp.exp(s - m)
        pr = pe * (1.0 / jnp.sum(pe, axis=-1, keepdims=True))
        att_ref[:, lanes_h] = _dot(pr.astype(BF16), v_ref[0, 0, :, lanes_h])
    h = h + _dot(att_ref[...].astype(BF16), wo_ref[...])

    u = _rms(h, gm_ref[...]).astype(BF16)
    acc = h
    for cb in range(0, MLP_HIDDEN, D_MODEL):
        up = jnp.maximum(_dot(u, wup_ref[:, cb:cb + D_MODEL]), 0.0)
        acc = acc + _dot((up * up).astype(BF16), wdn_ref[cb:cb + D_MODEL, :])
    return acc


def _tail_kernel(h_ref, m_ref, k_ref, v_ref, wmix_ref, gx_ref, wq_ref, qg_ref, wo_ref, gm_ref,
                 wup_ref, wdn_ref, gn_ref, wdt_ref, o_ref, u_ref, dt_ref, att_ref):
    h = _tail_body(h_ref, m_ref, k_ref, v_ref, wmix_ref, gx_ref, wq_ref, qg_ref, wo_ref, gm_ref,
                   wup_ref, wdn_ref, att_ref)
    o_ref[...] = h
    u = _rms(h, gn_ref[...]).astype(BF16)
    u_ref[...] = u
    dt_ref[...] = _dot(u, wdt_ref[...])


def _tail_last_kernel(h_ref, m_ref, k_ref, v_ref, wmix_ref, gx_ref, wq_ref, qg_ref, wo_ref, gm_ref,
                      wup_ref, wdn_ref, o_ref, att_ref):
    o_ref[...] = _tail_body(h_ref, m_ref, k_ref, v_ref, wmix_ref, gx_ref, wq_ref, qg_ref, wo_ref,
                            gm_ref, wup_ref, wdn_ref, att_ref)


def _tail(h2d, merged, k_mem, v_mem, layer, consts, next_consts, batch, seq, tl):
    nl = seq // tl
    t = batch * seq
    act = lambda b, l: (b * nl + l, 0)
    kv_spec = pl.BlockSpec((1, 1, MEM_LEN, D_MODEL), lambda b, l: (layer, b, 0, 0))
    in_specs = ([pl.BlockSpec((tl, D_MODEL), act), pl.BlockSpec((tl, D_MODEL), act), kv_spec, kv_spec]
                + [_const_spec(a.shape) for a in consts])
    out_specs = [pl.BlockSpec((tl, D_MODEL), act)]
    out_shape = [jax.ShapeDtypeStruct((t, D_MODEL), F32)]
    if next_consts is None:
        body, extra = _tail_last_kernel, ()
    else:
        body, extra = _tail_kernel, tuple(next_consts)
        in_specs += [_const_spec(a.shape) for a in extra]
        out_specs += [pl.BlockSpec((tl, D_MODEL), act), pl.BlockSpec((tl, DT_PAD), act)]
        out_shape += [jax.ShapeDtypeStruct((t, D_MODEL), BF16), jax.ShapeDtypeStruct((t, DT_PAD), F32)]
    return pl.pallas_call(
        body,
        grid=(batch, nl),
        in_specs=in_specs,
        out_specs=out_specs,
        out_shape=out_shape,
        scratch_shapes=[pltpu.VMEM((tl, D_MODEL), F32)],
        compiler_params=_cparams(("parallel", "arbitrary")),
        name="tail",
    )(h2d, merged, k_mem, v_mem, *consts, *extra)


def _head_indicator(width, head_dim):
    idx = np.arange(width) // head_dim
    return (idx[:, None] == idx[None, :]).astype(np.float32)


def _tile_sizes(seq):
    pick = lambda cands: next(c for c in cands if seq % c == 0)
    return dict(tm=pick((1024, 512, 256, 128)), tl_mix=pick((256, 128)), tl_tail=pick((512, 256, 128)))


def kernel(x, mem, rel_table, norm_mix, w_in, gate_bias, conv_dw_w, conv_dw_b, conv_ln_g, conv_ln_b, w_conv_out, ssd_conv_w, ssd_conv_b, ssd_dt_bias, ssd_A_log, ssd_D, ssd_norm_g, w_ssd_out, attn_q_norm, attn_k_norm, attn_sinks, w_attn_out, w_mix_out, norm_xattn, norm_mem, w_xq, w_xkv, xattn_q_norm, xattn_k_norm, w_xo, norm_mlp, w_mlp_up, w_mlp_down):
    batch, seq, _ = x.shape
    depth = w_in.shape[0]
    ts = _tile_sizes(seq)
    assert seq % ATTN_BLOCK == 0 and seq % SSD_CHUNK == 0

    w_p = jnp.concatenate([
        w_in[:, :, OFF_GATE:IN_COLS], w_in[:, :, OFF_XBC:OFF_DT], w_in[:, :, OFF_CONV:OFF_Z],
        w_in[:, :, OFF_Z:OFF_XBC], w_in[:, :, OFF_Q:OFF_GATE]], axis=2).astype(BF16)
    w_dt = jnp.pad(w_in[:, :, OFF_DT:OFF_Q], ((0, 0), (0, 0), (0, DT_PAD - SSD_HEADS))).astype(BF16)
    pad_h = ((0, 0), (0, DT_PAD - SSD_HEADS))
    dt_bias = jnp.pad(ssd_dt_bias, pad_h)[:, None, :]
    a_row = -jnp.exp(jnp.pad(ssd_A_log, pad_h))[:, None, :]
    d_exp = jnp.repeat(ssd_D, SSD_HEAD_DIM, axis=1)[:, None, :]

    e_np = np.zeros((DT_PAD, SSD_INNER), np.float32)
    e_np[np.arange(SSD_INNER) // SSD_HEAD_DIM, np.arange(SSD_INNER)] = 1.0
    e_mat = jnp.asarray(e_np, BF16)
    tril = jnp.asarray(np.tril(np.ones((SSD_CHUNK, SSD_CHUNK), np.float32)), BF16)
    ek = jnp.asarray(_head_indicator(ATTN_KV_DIM, ATTN_HEAD_DIM), BF16)
    rep_np = np.zeros((ATTN_KV_DIM, ATTN_DIM), np.float32)
    lanes = np.arange(ATTN_DIM)
    rep_np[(lanes // ATTN_GROUP_W) * ATTN_HEAD_DIM + lanes % ATTN_HEAD_DIM, lanes] = 1.0
    rep = jnp.asarray(rep_np, BF16)

    bias = _band_bias(rel_table)
    k_mem, v_mem = _mem_kv(mem, norm_mem[:, None, :], w_xkv.astype(BF16), xattn_k_norm[:, None, :])

    row = lambda a, i: a[i][None, :]
    h = x.reshape(batch * seq, D_MODEL)
    u, dt = _norm_dt(h, row(norm_mix, 0), w_dt[0], ts["tm"])
    for i in range(depth):
        p = _inproj(u, w_p[i], ts["tm"], P_COLS // 4)
        conv_consts = (conv_dw_w[i], row(conv_dw_b, i), row(conv_ln_g, i), row(conv_ln_b, i),
                       w_conv_out[i].astype(BF16))
        ssd_consts = (ssd_conv_w[i], row(ssd_conv_b, i), dt_bias[i], a_row[i], d_exp[i],
                      row(ssd_norm_g, i), w_ssd_out[i].astype(BF16), e_mat, tril)
        swa_consts = (bias, jnp.tile(attn_q_norm[i], ATTN_HEADS)[None, :],
                      jnp.tile(attn_k_norm[i], ATTN_KV_HEADS)[None, :], ek, rep,
                      w_attn_out[i].astype(BF16))
        merged = _mixers(p, dt, attn_sinks[i], gate_bias[i], conv_consts, ssd_consts, swa_consts,
                         batch, seq, ts["tl_mix"])
        tail_consts = (w_mix_out[i].astype(BF16), row(norm_xattn, i), w_xq[i].astype(BF16),
                       row(xattn_q_norm, i), w_xo[i].astype(BF16), row(norm_mlp, i),
                       w_mlp_up[i].astype(BF16), w_mlp_down[i].astype(BF16))
        if i + 1 < depth:
            h, u, dt = _tail(h, merged, k_mem, v_mem, i, tail_consts, (row(norm_mix, i + 1), w_dt[i + 1]),
                             batch, seq, ts["tl_tail"])
        else:
            (h,) = _tail(h, merged, k_mem, v_mem, i, tail_consts, None, batch, seq, ts["tl_tail"])
    return h.reshape(batch, seq, D_MODEL)
```

```python
import math

import numpy as np
import jax
import jax.numpy as jnp
from jax import lax
from jax.experimental import pallas as pl
from jax.experimental.pallas import tpu as pltpu

F32 = jnp.float32
BF16 = jnp.bfloat16

D_MODEL = 1024
MEM_LEN = 256
EPS = 1e-6
NEG_INF = -1e30

CONV_DIM = D_MODEL
CONV_KERNEL = 31

SSD_INNER = 2 * D_MODEL
SSD_HEAD_DIM = 64
SSD_HEADS = SSD_INNER // SSD_HEAD_DIM
SSD_GROUPS = 4
SSD_STATE = 128
SSD_CONV = 4
SSD_CHUNK = 128
SSD_BC = SSD_GROUPS * SSD_STATE
SSD_XBC = SSD_INNER + 2 * SSD_BC
SSD_GROUP_W = SSD_INNER // SSD_GROUPS

ATTN_HEADS = 16
ATTN_KV_HEADS = 4
ATTN_HEAD_DIM = 64
ATTN_DIM = ATTN_HEADS * ATTN_HEAD_DIM
ATTN_KV_DIM = ATTN_KV_HEADS * ATTN_HEAD_DIM
ATTN_REP = ATTN_HEADS // ATTN_KV_HEADS
ATTN_WINDOW = 128
ATTN_BLOCK = 128
ATTN_GROUP_W = ATTN_REP * ATTN_HEAD_DIM

REL_BUCKETS = 32
REL_MAX_DIST = 128

XATTN_HEADS = 4
XATTN_HEAD_DIM = D_MODEL // XATTN_HEADS

N_BRANCH = 3
MLP_HIDDEN = 4 * D_MODEL

OFF_CONV = 0
OFF_Z = OFF_CONV + 2 * CONV_DIM
OFF_XBC = OFF_Z + SSD_INNER
OFF_DT = OFF_XBC + SSD_XBC
OFF_Q = OFF_DT + SSD_HEADS
OFF_K = OFF_Q + ATTN_DIM
OFF_V = OFF_K + ATTN_KV_DIM
OFF_GATE = OFF_V + ATTN_KV_DIM
IN_COLS = OFF_GATE + N_BRANCH * D_MODEL

LANES = 128
SUBLANES = 8

P_GATE = 0
P_XBC = P_GATE + N_BRANCH * D_MODEL
P_CONV = P_XBC + SSD_XBC
P_Z = P_CONV + 2 * CONV_DIM
P_Q = P_Z + SSD_INNER
P_K = P_Q + ATTN_DIM
P_V = P_K + ATTN_KV_DIM
P_COLS = P_V + ATTN_KV_DIM
DT_PAD = LANES

VMEM_LIMIT = 56 * 1024 * 1024


def _cparams(semantics):
    return pltpu.CompilerParams(dimension_semantics=semantics, vmem_limit_bytes=VMEM_LIMIT)


def _const_spec(shape):
    nd = len(shape)
    return pl.BlockSpec(shape, lambda *_: (0,) * nd, pipeline_mode=pl.Buffered(1))


NEG_LOG2_E = -1.4426950408889634


def _sigmoid(x):
    return 1.0 / (1.0 + jnp.exp2(x * NEG_LOG2_E))


def _silu(x):
    return x * _sigmoid(x)


def _rms(x, g):
    return x * lax.rsqrt(jnp.mean(x * x, axis=-1, keepdims=True) + EPS) * g


def _dot(a, b):
    return jnp.dot(a, b, preferred_element_type=F32)


def _dot_nt(a, b):
    return lax.dot_general(a, b, (((1,), (1,)), ((), ())), preferred_element_type=F32)


def _shifted_taps(buf_ref, w_ref, offsets, r, rows, cols):
    acc = None
    for s in range(SUBLANES):
        group = [(j, o) for j, o in enumerate(offsets) if o % SUBLANES == s]
        if not group:
            continue
        n = rows if s == 0 else rows + SUBLANES
        t_s = None
        for j, o in group:
            base = r + o - s
            term = w_ref[j:j + 1, cols] * buf_ref[base:base + n, cols]
            t_s = term if t_s is None else t_s + term
        part = t_s if s == 0 else _shift_rows(t_s, s, rows)
        acc = part if acc is None else acc + part
    return acc


def _shift_rows(t, s, rows):
    tiles = [t[i:i + SUBLANES] for i in range(0, rows + SUBLANES, SUBLANES)]
    upper = lax.broadcasted_iota(jnp.int32, tiles[0].shape, 0) >= s
    return jnp.concatenate([pltpu.roll(jnp.where(upper, a, b), SUBLANES - s, axis=0)
                            for a, b in zip(tiles[:-1], tiles[1:])], axis=0)


def _split3(x):
    hi = x.astype(BF16)
    r1 = x - hi.astype(F32)
    mid = r1.astype(BF16)
    lo = (r1 - mid.astype(F32)).astype(BF16)
    return hi, mid, lo


def _norm_dt_kernel(x_ref, g_ref, wdt_ref, u_ref, dt_ref):
    u = _rms(x_ref[...], g_ref[...]).astype(BF16)
    u_ref[...] = u
    dt_ref[...] = _dot(u, wdt_ref[...])


def _norm_dt(h2d, g, wdt, tm):
    t = h2d.shape[0]
    return pl.pallas_call(
        _norm_dt_kernel,
        grid=(t // tm,),
        in_specs=[pl.BlockSpec((tm, D_MODEL), lambda i: (i, 0)),
                  _const_spec((1, D_MODEL)),
                  _const_spec((D_MODEL, DT_PAD))],
        out_specs=[pl.BlockSpec((tm, D_MODEL), lambda i: (i, 0)),
                   pl.BlockSpec((tm, DT_PAD), lambda i: (i, 0))],
        out_shape=[jax.ShapeDtypeStruct((t, D_MODEL), BF16),
                   jax.ShapeDtypeStruct((t, DT_PAD), F32)],
        compiler_params=_cparams(("parallel",)),
        name="norm_dt",
    )(h2d, g, wdt)


def _inproj_kernel(u_ref, w_ref, p_ref):
    p_ref[...] = _dot(u_ref[...], w_ref[...]).astype(BF16)


def _inproj(u, w, tm, tn):
    t = u.shape[0]
    return pl.pallas_call(
        _inproj_kernel,
        grid=(t // tm, P_COLS // tn),
        in_specs=[pl.BlockSpec((tm, D_MODEL), lambda i, j: (i, 0)),
                  pl.BlockSpec((D_MODEL, tn), lambda i, j: (0, j))],
        out_specs=pl.BlockSpec((tm, tn), lambda i, j: (i, j)),
        out_shape=jax.ShapeDtypeStruct((t, P_COLS), BF16),
        compiler_params=_cparams(("parallel", "arbitrary")),
        name="inproj",
    )(u, w)


CONV_HALO = 32
CONV_ROWS = 128
SSD_TAIL = SUBLANES
SSD_CONV_ROWS = 64
SSD_CONV_COLS = 256


def _round_robin(*streams):
    total = max(len(s) for s in streams)
    order = []
    for step in range(total):
        for s in streams:
            lo = step * len(s) // total
            hi = (step + 1) * len(s) // total
            order.extend(s[lo:hi])
    return order


def _gate(gates_ref, gb_ref, branch):
    cols = slice(branch * D_MODEL, (branch + 1) * D_MODEL)
    return _sigmoid(gates_ref[:, cols].astype(F32) + gb_ref[branch:branch + 1, :])


def _conv_part(cur_ref, halo_ref, dww_ref, dwb_ref, lng_ref, lnb_ref, w_ref, buf_ref, acc_ref):
    tl = cur_ref.shape[0]
    c = CONV_DIM
    cur = cur_ref[...].astype(F32)
    buf_ref[CONV_HALO:CONV_HALO + tl, :] = cur[:, :c] * _sigmoid(cur[:, c:])
    hal = halo_ref[...].astype(F32)
    hglu = hal[:, :c] * _sigmoid(hal[:, c:])
    buf_ref[0:CONV_HALO, :] = jnp.where(pl.program_id(1) > 0, hglu, 0.0)

    first = CONV_HALO - (CONV_KERNEL - 1)
    offsets = [first + j for j in range(CONV_KERNEL)]

    def taps(r, cb):
        cols = slice(cb, cb + LANES)
        acc_ref[r:r + CONV_ROWS, cols] = (
            _shifted_taps(buf_ref, dww_ref, offsets, r, CONV_ROWS, cols) + dwb_ref[:, cols])

    def finish():
        y = acc_ref[...]
        mu = jnp.mean(y, axis=-1, keepdims=True)
        yc = y - mu
        yn = yc * lax.rsqrt(jnp.mean(yc * yc, axis=-1, keepdims=True) + EPS)
        yn = _silu(yn * lng_ref[...] + lnb_ref[...])
        return _dot(yn.astype(BF16), w_ref[...])

    units = [(lambda r=r, cb=cb: taps(r, cb)) for r in range(0, tl, CONV_ROWS) for cb in range(0, c, LANES)]
    return units, finish


def _ssd_part(z_ref, xbc_ref, dt_ref, cw_ref, cb_ref, dtb_ref, a_ref, dexp_ref, ng_ref, w_ref,
              e_ref, tril_ref, xb_ref, xs_ref, bm_ref, cm_ref, y_ref, st_ref):
    tl = z_ref.shape[0]
    q = SSD_CHUNK
    gw = SSD_GROUP_W

    @pl.when(pl.program_id(1) == 0)
    def _():
        st_ref[...] = jnp.zeros_like(st_ref)
        xb_ref[0:SSD_TAIL, :] = jnp.zeros((SSD_TAIL, SSD_XBC), F32)

    xb_ref[SSD_TAIL:SSD_TAIL + tl, :] = xbc_ref[...].astype(F32)
    first = SSD_TAIL - (SSD_CONV - 1)
    offsets = [first + j for j in range(SSD_CONV)]
    for r in range(0, tl, SSD_CONV_ROWS):
        rows = slice(r, r + SSD_CONV_ROWS)
        for cbk in range(0, SSD_XBC, SSD_CONV_COLS):
            cols = slice(cbk, cbk + SSD_CONV_COLS)
            act = _silu(_shifted_taps(xb_ref, cw_ref, offsets, r, SSD_CONV_ROWS, cols) + cb_ref[:, cols])
            if cbk < SSD_INNER:
                xs_ref[rows, cols] = act
            elif cbk < SSD_INNER + SSD_BC:
                bm_ref[rows, cbk - SSD_INNER:cbk - SSD_INNER + SSD_CONV_COLS] = act
            else:
                off = cbk - SSD_INNER - SSD_BC
                cm_ref[rows, off:off + SSD_CONV_COLS] = act
    xb_ref[0:SSD_TAIL, :] = xb_ref[tl:tl + SSD_TAIL, :]

    x = dt_ref[...] + dtb_ref[...]
    dt_all = jnp.maximum(x, 0.0) + jnp.log1p(jnp.exp(-jnp.abs(x)))

    row = lax.broadcasted_iota(jnp.int32, (q, q), 0)
    col = lax.broadcasted_iota(jnp.int32, (q, q), 1)
    tri = row >= col
    lane = lax.broadcasted_iota(jnp.int32, (q, LANES), 1)
    lo_half = lane < SSD_HEAD_DIM
    tril = tril_ref[...]

    ctx = {}

    def chunk_head(r0):
        rows = slice(r0, r0 + q)
        dt_c = dt_all[rows]
        da = dt_c * a_ref[...]
        hi, mid, lo = _split3(da)
        cs = _dot(tril, hi) + _dot(tril, mid) + _dot(tril, lo)
        adj_t = (cs - jnp.log(dt_c)).T
        cs_last = cs[q - 1:q, :]
        w_dec = jnp.exp(cs_last - cs) * dt_c
        dfs = jnp.exp(cs)
        cd = jnp.exp(cs_last)
        cd_hi = cd.astype(BF16)
        cd_lo = (cd - cd_hi.astype(F32)).astype(BF16)
        stack = jnp.concatenate([
            w_dec.astype(BF16), dfs.astype(BF16),
            jnp.broadcast_to(cd_hi, (SUBLANES * 2, LANES)),
            jnp.broadcast_to(cd_lo, (SUBLANES * 2, LANES))], axis=0)
        ex = _dot(stack, e_ref[...])
        w_exp = ex[0:q]
        dfs_exp = ex[q:2 * q]
        cd_exp = ex[2 * q:2 * q + 1] + ex[2 * q + 2 * SUBLANES:2 * q + 2 * SUBLANES + 1]

        ctx[r0] = (cs, adj_t, w_exp, dfs_exp, cd_exp)

    def chunk_body(r0):
        rows = slice(r0, r0 + q)
        cs, adj_t, w_exp, dfs_exp, cd_exp = ctx[r0]
        groups = range(SSD_GROUPS)
        pairs = range(gw // LANES)
        lanes = [slice(g * gw, (g + 1) * gw) for g in groups]
        xs_g = [xs_ref[rows, lanes[g]] for g in groups]
        b_g = [bm_ref[rows, g * SSD_STATE:(g + 1) * SSD_STATE] for g in groups]
        c_g = [cm_ref[rows, g * SSD_STATE:(g + 1) * SSD_STATE].astype(BF16) for g in groups]
        st_g = [st_ref[:, lanes[g]] for g in groups]
        cbm = [_dot_nt(c_g[g], b_g[g].astype(BF16)) for g in groups]
        y_off = [_dot(c_g[g], st_g[g].astype(BF16)) * dfs_exp[:, lanes[g]] for g in groups]
        lhs = {}
        for g in groups:
            for pr in pairs:
                h0 = (g * gw + pr * LANES) // SSD_HEAD_DIM
                sc = []
                for h in (h0, h0 + 1):
                    dec = jnp.where(tri, jnp.exp(cs[:, h:h + 1] - adj_t[h:h + 1, :]), 0.0)
                    sc.append((cbm[g] * dec).astype(BF16))
                lhs[g, pr] = jnp.concatenate(sc, axis=1)
        for g in groups:
            for pr in pairs:
                lanes_p = slice(pr * LANES, (pr + 1) * LANES)
                xp = xs_g[g][:, lanes_p].astype(BF16)
                zero = jnp.zeros_like(xp)
                rhs = jnp.concatenate([jnp.where(lo_half, xp, zero),
                                       jnp.where(lo_half, zero, xp)], axis=0)
                y_ref[rows, g * gw + pr * LANES:g * gw + (pr + 1) * LANES] = (
                    y_off[g][:, lanes_p] + _dot(lhs[g, pr], rhs))
        for g in groups:
            xw = (xs_g[g] * w_exp[:, lanes[g]]).astype(BF16)
            st_ref[:, lanes[g]] = st_g[g] * cd_exp[:, lanes[g]] + _dot(b_g[g].T.astype(BF16), xw)

    def finish():
        y = y_ref[...] + xs_ref[...] * dexp_ref[...]
        y = y * _silu(z_ref[...].astype(F32))
        parts = []
        for g in range(SSD_GROUPS):
            yg = y[:, g * gw:(g + 1) * gw]
            parts.append(yg * lax.rsqrt(jnp.mean(yg * yg, axis=-1, keepdims=True) + EPS))
        yn = jnp.concatenate(parts, axis=1) * ng_ref[...]
        return _dot(yn.astype(BF16), w_ref[...])

    for r0 in range(0, tl, q):
        chunk_head(r0)
    units = [(lambda r0=r0: chunk_body(r0)) for r0 in range(0, tl, q)]
    return units, finish


def _swa_part(sink_ref, q_ref, kc_ref, vc_ref, kp_ref, vp_ref, bias_ref, qg_ref, kg_ref, ek_ref,
              rep_ref, w_ref, att_ref):
    tq = q_ref.shape[0]
    blk = ATTN_BLOCK
    gwid = ATTN_GROUP_W
    inv_d = 1.0 / ATTN_HEAD_DIM

    qf = q_ref[...].astype(F32)
    q_sq = (qf * qf).astype(BF16)
    q_ss = jnp.concatenate([_dot(q_sq[:, g * gwid:(g + 1) * gwid], ek_ref[...])
                            for g in range(ATTN_KV_HEADS)], axis=1)
    qn = (qf * lax.rsqrt(q_ss * inv_d + EPS) * qg_ref[...] * (ATTN_HEAD_DIM ** -0.5)).astype(BF16)

    kf = jnp.concatenate([kp_ref[...], kc_ref[...]], axis=0).astype(F32)
    k_ss = _dot((kf * kf).astype(BF16), ek_ref[...])
    kn = (kf * lax.rsqrt(k_ss * inv_d + EPS) * kg_ref[...]).astype(BF16)
    k_rep = _dot(kn, rep_ref[...]).astype(BF16)
    v_all = jnp.concatenate([vp_ref[...], vc_ref[...]], axis=0)
    v_rep = _dot(v_all, rep_ref[...]).astype(BF16)

    lane = lax.broadcasted_iota(jnp.int32, (1, gwid), 1) // ATTN_HEAD_DIM
    key_col = lax.broadcasted_iota(jnp.int32, (blk, 2 * blk), 1)
    first_tile = pl.program_id(1) == 0

    cur_side = (lax.broadcasted_iota(jnp.int32, (blk, blk), 1)
                <= lax.broadcasted_iota(jnp.int32, (blk, blk), 0))

    def block(n):
        r0 = n * blk
        folded = []
        for g in range(ATTN_KV_HEADS):
            lanes_g = slice(g * gwid, (g + 1) * gwid)
            q_g = qn[r0:r0 + blk, lanes_g]
            k_g = k_rep[r0:r0 + 2 * blk, lanes_g]
            for hh in range(ATTN_REP):
                h = g * ATTN_REP + hh
                s = _dot_nt(jnp.where(lane == hh, q_g, jnp.zeros_like(q_g)), k_g) + bias_ref[h]
                if n == 0:
                    s = jnp.where(jnp.logical_and(first_tile, key_col < blk), NEG_INF, s)
                folded.append(jnp.maximum(s[:, :blk], s[:, blk:]))
        probs = []
        for h in range(ATTN_HEADS):
            sink = sink_ref[h]
            m = jnp.maximum(jnp.max(folded[h], axis=-1, keepdims=True), sink)
            pe = jnp.exp(folded[h] - m)
            den = jnp.sum(pe, axis=-1, keepdims=True) + jnp.exp(sink - m)
            pr = pe * (1.0 / den)
            probs.append(jnp.concatenate([jnp.where(cur_side, 0.0, pr), jnp.where(cur_side, pr, 0.0)],
                                         axis=1).astype(BF16))
        for g in range(ATTN_KV_HEADS):
            lanes_g = slice(g * gwid, (g + 1) * gwid)
            v_g = v_rep[r0:r0 + 2 * blk, lanes_g]
            v_blocks = [jnp.where(lane == hh, v_g, jnp.zeros_like(v_g)) for hh in range(ATTN_REP)]
            att_ref[r0:r0 + blk, lanes_g] = _dot(
                jnp.concatenate(probs[g * ATTN_REP:(g + 1) * ATTN_REP], axis=1),
                jnp.concatenate(v_blocks, axis=0))

    def finish():
        return _dot(att_ref[...].astype(BF16), w_ref[...])

    units = [(lambda n=n: block(n)) for n in range(tq // blk)]
    return units, finish


def _mixers_kernel(sink_ref, p_ref, halo_ref, kvp_ref, dt_ref, gb_ref,
                   dww_ref, dwb_ref, lng_ref, lnb_ref, wa_ref,
                   cw_ref, cb_ref, dtb_ref, a_ref, dexp_ref, ng_ref, wb_ref, e_ref, tril_ref,
                   bias_ref, qg_ref, kg_ref, ek_ref, rep_ref, wc_ref,
                   o_ref,
                   cbuf_ref, cacc_ref, xb_ref, xs_ref, bm_ref, cm_ref, y_ref, st_ref, att_ref):
    seg = lambda off, width: p_ref.at[:, off:off + width]
    gates_ref = seg(P_GATE, N_BRANCH * D_MODEL)
    cur_ref = seg(P_CONV, 2 * CONV_DIM)
    z_ref = seg(P_Z, SSD_INNER)
    xbc_ref = seg(P_XBC, SSD_XBC)
    q_ref = seg(P_Q, ATTN_DIM)
    kc_ref = seg(P_K, ATTN_KV_DIM)
    vc_ref = seg(P_V, ATTN_KV_DIM)
    kp_ref = kvp_ref.at[:, 0:ATTN_KV_DIM]
    vp_ref = kvp_ref.at[:, ATTN_KV_DIM:2 * ATTN_KV_DIM]
    conv_units, conv_finish = _conv_part(cur_ref, halo_ref, dww_ref, dwb_ref, lng_ref, lnb_ref, wa_ref,
                                         cbuf_ref, cacc_ref)
    ssd_units, ssd_finish = _ssd_part(z_ref, xbc_ref, dt_ref, cw_ref, cb_ref, dtb_ref, a_ref, dexp_ref,
                                      ng_ref, wb_ref, e_ref, tril_ref, xb_ref, xs_ref, bm_ref, cm_ref,
                                      y_ref, st_ref)
    swa_units, swa_finish = _swa_part(sink_ref, q_ref, kc_ref, vc_ref, kp_ref, vp_ref, bias_ref, qg_ref,
                                      kg_ref, ek_ref, rep_ref, wc_ref, att_ref)
    for unit in _round_robin(ssd_units, swa_units):
        unit()
    y_c = _gate(gates_ref, gb_ref, 2) * swa_finish()
    y_b = _gate(gates_ref, gb_ref, 1) * ssd_finish()
    for unit in conv_units:
        unit()
    y_a = _gate(gates_ref, gb_ref, 0) * conv_finish()
    o_ref[...] = (y_a + y_b + y_c).astype(BF16)


def _mixers(p, dt, sinks, gb, conv_consts, ssd_consts, swa_consts, batch, seq, tl):
    nl = seq // tl
    cw = 2 * CONV_DIM
    kvw = 2 * ATTN_KV_DIM
    assert P_V == P_K + ATTN_KV_DIM and P_K % kvw == 0 and P_CONV % cw == 0

    def prev_rows(rows, width, col_off):
        per_tile = tl // rows
        return pl.BlockSpec(
            (rows, width), lambda b, l: (jnp.maximum((b * nl + l) * per_tile - 1, 0), col_off // width))

    consts = (gb,) + tuple(conv_consts) + tuple(ssd_consts) + tuple(swa_consts)
    in_specs = (
        [pl.BlockSpec(memory_space=pltpu.SMEM),
         pl.BlockSpec((tl, P_COLS), lambda b, l: (b * nl + l, 0)),
         prev_rows(CONV_HALO, cw, P_CONV),
         prev_rows(ATTN_BLOCK, kvw, P_K),
         pl.BlockSpec((tl, DT_PAD), lambda b, l: (b * nl + l, 0))]
        + [_const_spec(a.shape) for a in consts])
    return pl.pallas_call(
        _mixers_kernel,
        grid=(batch, nl),
        in_specs=in_specs,
        out_specs=pl.BlockSpec((tl, D_MODEL), lambda b, l: (b * nl + l, 0)),
        out_shape=jax.ShapeDtypeStruct((batch * seq, D_MODEL), BF16),
        scratch_shapes=[
            pltpu.VMEM((CONV_HALO + tl, CONV_DIM), F32),
            pltpu.VMEM((tl, CONV_DIM), F32),
            pltpu.VMEM((SSD_TAIL + tl, SSD_XBC), F32),
            pltpu.VMEM((tl, SSD_INNER), F32),
            pltpu.VMEM((tl, SSD_BC), F32),
            pltpu.VMEM((tl, SSD_BC), F32),
            pltpu.VMEM((tl, SSD_INNER), F32),
            pltpu.VMEM((SSD_STATE, SSD_INNER), F32),
            pltpu.VMEM((tl, ATTN_DIM), F32),
        ],
        compiler_params=_cparams(("parallel", "arbitrary")),
        name="mixers",
    )(sinks, p, p, p, dt, *consts)


def _band_buckets():
    qi = np.arange(ATTN_BLOCK)[:, None] + ATTN_BLOCK
    kj = np.arange(2 * ATTN_BLOCK)[None, :]
    dist = qi - kj
    max_exact = REL_BUCKETS // 2
    d = np.maximum(dist, 1).astype(np.float32)
    large = max_exact + (np.log(d / np.float32(max_exact)) / np.float32(math.log(REL_MAX_DIST / max_exact))
                         * np.float32(REL_BUCKETS - max_exact)).astype(np.int32)
    large = np.minimum(large, REL_BUCKETS - 1)
    bucket = np.where(dist < max_exact, np.maximum(dist, 0), large).astype(np.int32)
    in_window = (dist >= 0) & (dist < ATTN_WINDOW)
    return bucket, in_window.astype(np.int32)


def _bias_kernel(tab_ref, bucket_ref, win_ref, o_ref):
    bucket = bucket_ref[...]
    win = win_ref[...] > 0
    for h in range(ATTN_HEADS):
        acc = jnp.zeros(bucket.shape, F32)
        for b in range(REL_BUCKETS):
            acc = jnp.where(bucket == b, tab_ref[b, h], acc)
        o_ref[h] = jnp.where(win, acc, NEG_INF)


def _band_bias(rel_table):
    bucket, win = _band_buckets()
    shape = (ATTN_BLOCK, 2 * ATTN_BLOCK)
    return pl.pallas_call(
        _bias_kernel,
        in_specs=[pl.BlockSpec(memory_space=pltpu.SMEM),
                  pl.BlockSpec(shape, lambda: (0, 0)),
                  pl.BlockSpec(shape, lambda: (0, 0))],
        out_specs=pl.BlockSpec((ATTN_HEADS,) + shape, lambda: (0, 0, 0)),
        out_shape=jax.ShapeDtypeStruct((ATTN_HEADS,) + shape, F32),
        name="band_bias",
    )(rel_table, jnp.asarray(bucket), jnp.asarray(win))


def _memkv_kernel(mem_ref, g_ref, w_ref, kg_ref, k_ref, v_ref):
    mn = _rms(mem_ref[0], g_ref[0]).astype(BF16)
    kv = _dot(mn, w_ref[0])
    parts = []
    for h in range(XATTN_HEADS):
        kh = kv[:, h * XATTN_HEAD_DIM:(h + 1) * XATTN_HEAD_DIM]
        parts.append(_rms(kh, kg_ref[0]))
    k_ref[0, 0] = jnp.concatenate(parts, axis=1).astype(BF16)
    v_ref[0, 0] = kv[:, D_MODEL:].astype(BF16)


def _mem_kv(mem, g, w, kg):
    depth = w.shape[0]
    batch = mem.shape[0]
    out = jax.ShapeDtypeStruct((depth, batch, MEM_LEN, D_MODEL), BF16)
    return pl.pallas_call(
        _memkv_kernel,
        grid=(depth, batch),
        in_specs=[
            pl.BlockSpec((1, MEM_LEN, D_MODEL), lambda i, b: (b, 0, 0)),
            pl.BlockSpec((1, 1, D_MODEL), lambda i, b: (i, 0, 0)),
            pl.BlockSpec((1, D_MODEL, 2 * D_MODEL), lambda i, b: (i, 0, 0)),
            pl.BlockSpec((1, 1, XATTN_HEAD_DIM), lambda i, b: (i, 0, 0)),
        ],
        out_specs=[pl.BlockSpec((1, 1, MEM_LEN, D_MODEL), lambda i, b: (i, b, 0, 0)),
                   pl.BlockSpec((1, 1, MEM_LEN, D_MODEL), lambda i, b: (i, b, 0, 0))],
        out_shape=[out, out],
        compiler_params=_cparams(("arbitrary", "arbitrary")),
        name="mem_kv",
    )(mem, g, w, kg)


def _tail_body(h_ref, m_ref, k_ref, v_ref, wmix_ref, gx_ref, wq_ref, qg_ref, wo_ref, gm_ref,
               wup_ref, wdn_ref, att_ref):
    h = h_ref[...] + _dot(m_ref[...], wmix_ref[...])

    qx = _dot(_rms(h, gx_ref[...]).astype(BF16), wq_ref[...])
    for hd in range(XATTN_HEADS):
        lanes_h = slice(hd * XATTN_HEAD_DIM, (hd + 1) * XATTN_HEAD_DIM)
        qh = (_rms(qx[:, lanes_h], qg_ref[...]) * (XATTN_HEAD_DIM ** -0.5)).astype(BF16)
        s = _dot_nt(qh, k_ref[0, 0, :, lanes_h])
        m = jnp.max(s, axis=-1, keepdims=True)
        pe = jnp.exp(s - m)
        pr = pe * (1.0 / jnp.sum(pe, axis=-1, keepdims=True))
        att_ref[:, lanes_h] = _dot(pr.astype(BF16), v_ref[0, 0, :, lanes_h])
    h = h + _dot(att_ref[...].astype(BF16), wo_ref[...])

    u = _rms(h, gm_ref[...]).astype(BF16)
    acc = h
    for cb in range(0, MLP_HIDDEN, D_MODEL):
        up = jnp.maximum(_dot(u, wup_ref[:, cb:cb + D_MODEL]), 0.0)
        acc = acc + _dot((up * up).astype(BF16), wdn_ref[cb:cb + D_MODEL, :])
    return acc


def _tail_kernel(h_ref, m_ref, k_ref, v_ref, wmix_ref, gx_ref, wq_ref, qg_ref, wo_ref, gm_ref,
                 wup_ref, wdn_ref, gn_ref, wdt_ref, o_ref, u_ref, dt_ref, att_ref):
    h = _tail_body(h_ref, m_ref, k_ref, v_ref, wmix_ref, gx_ref, wq_ref, qg_ref, wo_ref, gm_ref,
                   wup_ref, wdn_ref, att_ref)
    o_ref[...] = h
    u = _rms(h, gn_ref[...]).astype(BF16)
    u_ref[...] = u
    dt_ref[...] = _dot(u, wdt_ref[...])


def _tail_last_kernel(h_ref, m_ref, k_ref, v_ref, wmix_ref, gx_ref, wq_ref, qg_ref, wo_ref, gm_ref,
                      wup_ref, wdn_ref, o_ref, att_ref):
    o_ref[...] = _tail_body(h_ref, m_ref, k_ref, v_ref, wmix_ref, gx_ref, wq_ref, qg_ref, wo_ref,
                            gm_ref, wup_ref, wdn_ref, att_ref)


def _tail(h2d, merged, k_mem, v_mem, layer, consts, next_consts, batch, seq, tl):
    nl = seq // tl
    t = batch * seq
    act = lambda b, l: (b * nl + l, 0)
    kv_spec = pl.BlockSpec((1, 1, MEM_LEN, D_MODEL), lambda b, l: (layer, b, 0, 0))
    in_specs = ([pl.BlockSpec((tl, D_MODEL), act), pl.BlockSpec((tl, D_MODEL), act), kv_spec, kv_spec]
                + [_const_spec(a.shape) for a in consts])
    out_specs = [pl.BlockSpec((tl, D_MODEL), act)]
    out_shape = [jax.ShapeDtypeStruct((t, D_MODEL), F32)]
    if next_consts is None:
        body, extra = _tail_last_kernel, ()
    else:
        body, extra = _tail_kernel, tuple(next_consts)
        in_specs += [_const_spec(a.shape) for a in extra]
        out_specs += [pl.BlockSpec((tl, D_MODEL), act), pl.BlockSpec((tl, DT_PAD), act)]
        out_shape += [jax.ShapeDtypeStruct((t, D_MODEL), BF16), jax.ShapeDtypeStruct((t, DT_PAD), F32)]
    return pl.pallas_call(
        body,
        grid=(batch, nl),
        in_specs=in_specs,
        out_specs=out_specs,
        out_shape=out_shape,
        scratch_shapes=[pltpu.VMEM((tl, D_MODEL), F32)],
        compiler_params=_cparams(("parallel", "arbitrary")),
        name="tail",
    )(h2d, merged, k_mem, v_mem, *consts, *extra)


def _head_indicator(width, head_dim):
    idx = np.arange(width) // head_dim
    return (idx[:, None] == idx[None, :]).astype(np.float32)


def _tile_sizes(seq):
    pick = lambda cands: next(c for c in cands if seq % c == 0)
    return dict(tm=pick((1024, 512, 256, 128)), tl_mix=pick((256, 128)), tl_tail=pick((512, 256, 128)))


def kernel(x, mem, rel_table, norm_mix, w_in, gate_bias, conv_dw_w, conv_dw_b, conv_ln_g, conv_ln_b, w_conv_out, ssd_conv_w, ssd_conv_b, ssd_dt_bias, ssd_A_log, ssd_D, ssd_norm_g, w_ssd_out, attn_q_norm, attn_k_norm, attn_sinks, w_attn_out, w_mix_out, norm_xattn, norm_mem, w_xq, w_xkv, xattn_q_norm, xattn_k_norm, w_xo, norm_mlp, w_mlp_up, w_mlp_down):
    batch, seq, _ = x.shape
    depth = w_in.shape[0]
    ts = _tile_sizes(seq)
    assert seq % ATTN_BLOCK == 0 and seq % SSD_CHUNK == 0

    w_p = jnp.concatenate([
        w_in[:, :, OFF_GATE:IN_COLS], w_in[:, :, OFF_XBC:OFF_DT], w_in[:, :, OFF_CONV:OFF_Z],
        w_in[:, :, OFF_Z:OFF_XBC], w_in[:, :, OFF_Q:OFF_GATE]], axis=2).astype(BF16)
    w_dt = jnp.pad(w_in[:, :, OFF_DT:OFF_Q], ((0, 0), (0, 0), (0, DT_PAD - SSD_HEADS))).astype(BF16)
    pad_h = ((0, 0), (0, DT_PAD - SSD_HEADS))
    dt_bias = jnp.pad(ssd_dt_bias, pad_h)[:, None, :]
    a_row = -jnp.exp(jnp.pad(ssd_A_log, pad_h))[:, None, :]
    d_exp = jnp.repeat(ssd_D, SSD_HEAD_DIM, axis=1)[:, None, :]

    e_np = np.zeros((DT_PAD, SSD_INNER), np.float32)
    e_np[np.arange(SSD_INNER) // SSD_HEAD_DIM, np.arange(SSD_INNER)] = 1.0
    e_mat = jnp.asarray(e_np, BF16)
    tril = jnp.asarray(np.tril(np.ones((SSD_CHUNK, SSD_CHUNK), np.float32)), BF16)
    ek = jnp.asarray(_head_indicator(ATTN_KV_DIM, ATTN_HEAD_DIM), BF16)
    rep_np = np.zeros((ATTN_KV_DIM, ATTN_DIM), np.float32)
    lanes = np.arange(ATTN_DIM)
    rep_np[(lanes // ATTN_GROUP_W) * ATTN_HEAD_DIM + lanes % ATTN_HEAD_DIM, lanes] = 1.0
    rep = jnp.asarray(rep_np, BF16)

    bias = _band_bias(rel_table)
    k_mem, v_mem = _mem_kv(mem, norm_mem[:, None, :], w_xkv.astype(BF16), xattn_k_norm[:, None, :])

    row = lambda a, i: a[i][None, :]
    h = x.reshape(batch * seq, D_MODEL)
    u, dt = _norm_dt(h, row(norm_mix, 0), w_dt[0], ts["tm"])
    for i in range(depth):
        p = _inproj(u, w_p[i], ts["tm"], P_COLS // 4)
        conv_consts = (conv_dw_w[i], row(conv_dw_b, i), row(conv_ln_g, i), row(conv_ln_b, i),
                       w_conv_out[i].astype(BF16))
        ssd_consts = (ssd_conv_w[i], row(ssd_conv_b, i), dt_bias[i], a_row[i], d_exp[i],
                      row(ssd_norm_g, i), w_ssd_out[i].astype(BF16), e_mat, tril)
        swa_consts = (bias, jnp.tile(attn_q_norm[i], ATTN_HEADS)[None, :],
                      jnp.tile(attn_k_norm[i], ATTN_KV_HEADS)[None, :], ek, rep,
                      w_attn_out[i].astype(BF16))
        merged = _mixers(p, dt, attn_sinks[i], gate_bias[i], conv_consts, ssd_consts, swa_consts,
                         batch, seq, ts["tl_mix"])
        tail_consts = (w_mix_out[i].astype(BF16), row(norm_xattn, i), w_xq[i].astype(BF16),
                       row(xattn_q_norm, i), w_xo[i].astype(BF16), row(norm_mlp, i),
                       w_mlp_up[i].astype(BF16), w_mlp_down[i].astype(BF16))
        if i + 1 < depth:
            h, u, dt = _tail(h, merged, k_mem, v_mem, i, tail_consts, (row(norm_mix, i + 1), w_dt[i + 1]),
                             batch, seq, ts["tl_tail"])
        else:
            (h,) = _tail(h, merged, k_mem, v_mem, i, tail_consts, None, batch, seq, ts["tl_tail"])
    return h.reshape(batch, seq, D_MODEL)
```

```python
import math

import numpy as np
import jax
import jax.numpy as jnp
from jax import lax
from jax.experimental import pallas as pl
from jax.experimental.pallas import tpu as pltpu

F32 = jnp.float32
BF16 = jnp.bfloat16

D_MODEL = 1024
MEM_LEN = 256
EPS = 1e-6
NEG_INF = -1e30

CONV_DIM = D_MODEL
CONV_KERNEL = 31

SSD_INNER = 2 * D_MODEL
SSD_HEAD_DIM = 64
SSD_HEADS = SSD_INNER // SSD_HEAD_DIM
SSD_GROUPS = 4
SSD_STATE = 128
SSD_CONV = 4
SSD_CHUNK = 128
SSD_BC = SSD_GROUPS * SSD_STATE
SSD_XBC = SSD_INNER + 2 * SSD_BC
SSD_GROUP_W = SSD_INNER // SSD_GROUPS

ATTN_HEADS = 16
ATTN_KV_HEADS = 4
ATTN_HEAD_DIM = 64
ATTN_DIM = ATTN_HEADS * ATTN_HEAD_DIM
ATTN_KV_DIM = ATTN_KV_HEADS * ATTN_HEAD_DIM
ATTN_REP = ATTN_HEADS // ATTN_KV_HEADS
ATTN_WINDOW = 128
ATTN_BLOCK = 128
ATTN_GROUP_W = ATTN_REP * ATTN_HEAD_DIM

REL_BUCKETS = 32
REL_MAX_DIST = 128

XATTN_HEADS = 4
XATTN_HEAD_DIM = D_MODEL // XATTN_HEADS

N_BRANCH = 3
MLP_HIDDEN = 4 * D_MODEL

OFF_CONV = 0
OFF_Z = OFF_CONV + 2 * CONV_DIM
OFF_XBC = OFF_Z + SSD_INNER
OFF_DT = OFF_XBC + SSD_XBC
OFF_Q = OFF_DT + SSD_HEADS
OFF_K = OFF_Q + ATTN_DIM
OFF_V = OFF_K + ATTN_KV_DIM
OFF_GATE = OFF_V + ATTN_KV_DIM
IN_COLS = OFF_GATE + N_BRANCH * D_MODEL

LANES = 128
SUBLANES = 8

P_GATE = 0
P_XBC = P_GATE + N_BRANCH * D_MODEL
P_CONV = P_XBC + SSD_XBC
P_Z = P_CONV + 2 * CONV_DIM
P_Q = P_Z + SSD_INNER
P_K = P_Q + ATTN_DIM
P_V = P_K + ATTN_KV_DIM
P_COLS = P_V + ATTN_KV_DIM
DT_PAD = LANES

VMEM_LIMIT = 56 * 1024 * 1024


def _cparams(semantics):
    return pltpu.CompilerParams(dimension_semantics=semantics, vmem_limit_bytes=VMEM_LIMIT)


def _const_spec(shape):
    nd = len(shape)
    return pl.BlockSpec(shape, lambda *_: (0,) * nd, pipeline_mode=pl.Buffered(1))


NEG_LOG2_E = -1.4426950408889634


def _sigmoid(x):
    return 1.0 / (1.0 + jnp.exp2(x * NEG_LOG2_E))


def _silu(x):
    return x * _sigmoid(x)


def _rms(x, g):
    return x * lax.rsqrt(jnp.mean(x * x, axis=-1, keepdims=True) + EPS) * g


def _dot(a, b):
    return jnp.dot(a, b, preferred_element_type=F32)


def _dot_nt(a, b):
    return lax.dot_general(a, b, (((1,), (1,)), ((), ())), preferred_element_type=F32)


def _shifted_taps(buf_ref, w_ref, offsets, r, rows, cols):
    acc = None
    for s in range(SUBLANES):
        group = [(j, o) for j, o in enumerate(offsets) if o % SUBLANES == s]
        if not group:
            continue
        n = rows if s == 0 else rows + SUBLANES
        t_s = None
        for j, o in group:
            base = r + o - s
            term = w_ref[j:j + 1, cols] * buf_ref[base:base + n, cols]
            t_s = term if t_s is None else t_s + term
        part = t_s if s == 0 else _shift_rows(t_s, s, rows)
        acc = part if acc is None else acc + part
    return acc


def _shift_rows(t, s, rows):
    tiles = [t[i:i + SUBLANES] for i in range(0, rows + SUBLANES, SUBLANES)]
    upper = lax.broadcasted_iota(jnp.int32, tiles[0].shape, 0) >= s
    return jnp.concatenate([pltpu.roll(jnp.where(upper, a, b), SUBLANES - s, axis=0)
                            for a, b in zip(tiles[:-1], tiles[1:])], axis=0)


def _split3(x):
    hi = x.astype(BF16)
    r1 = x - hi.astype(F32)
    mid = r1.astype(BF16)
    lo = (r1 - mid.astype(F32)).astype(BF16)
    return hi, mid, lo


def _norm_dt_kernel(x_ref, g_ref, wdt_ref, u_ref, dt_ref):
    u = _rms(x_ref[...], g_ref[...]).astype(BF16)
    u_ref[...] = u
    dt_ref[...] = _dot(u, wdt_ref[...])


def _norm_dt(h2d, g, wdt, tm):
    t = h2d.shape[0]
    return pl.pallas_call(
        _norm_dt_kernel,
        grid=(t // tm,),
        in_specs=[pl.BlockSpec((tm, D_MODEL), lambda i: (i, 0)),
                  _const_spec((1, D_MODEL)),
                  _const_spec((D_MODEL, DT_PAD))],
        out_specs=[pl.BlockSpec((tm, D_MODEL), lambda i: (i, 0)),
                   pl.BlockSpec((tm, DT_PAD), lambda i: (i, 0))],
        out_shape=[jax.ShapeDtypeStruct((t, D_MODEL), BF16),
                   jax.ShapeDtypeStruct((t, DT_PAD), F32)],
        compiler_params=_cparams(("parallel",)),
        name="norm_dt",
    )(h2d, g, wdt)


def _inproj_kernel(u_ref, w_ref, p_ref):
    p_ref[...] = _dot(u_ref[...], w_ref[...]).astype(BF16)


def _inproj(u, w, tm, tn):
    t = u.shape[0]
    return pl.pallas_call(
        _inproj_kernel,
        grid=(t // tm, P_COLS // tn),
        in_specs=[pl.BlockSpec((tm, D_MODEL), lambda i, j: (i, 0)),
                  pl.BlockSpec((D_MODEL, tn), lambda i, j: (0, j))],
        out_specs=pl.BlockSpec((tm, tn), lambda i, j: (i, j)),
        out_shape=jax.ShapeDtypeStruct((t, P_COLS), BF16),
        compiler_params=_cparams(("parallel", "arbitrary")),
        name="inproj",
    )(u, w)


CONV_HALO = 32
CONV_ROWS = 128
SSD_TAIL = SUBLANES
SSD_CONV_ROWS = 64
SSD_CONV_COLS = 256


def _round_robin(*streams):
    total = max(len(s) for s in streams)
    order = []
    for step in range(total):
        for s in streams:
            lo = step * len(s) // total
            hi = (step + 1) * len(s) // total
            order.extend(s[lo:hi])
    return order


def _gate(gates_ref, gb_ref, branch):
    cols = slice(branch * D_MODEL, (branch + 1) * D_MODEL)
    return _sigmoid(gates_ref[:, cols].astype(F32) + gb_ref[branch:branch + 1, :])


def _conv_part(cur_ref, halo_ref, dww_ref, dwb_ref, lng_ref, lnb_ref, w_ref, buf_ref, acc_ref):
    tl = cur_ref.shape[0]
    c = CONV_DIM
    cur = cur_ref[...].astype(F32)
    buf_ref[CONV_HALO:CONV_HALO + tl, :] = cur[:, :c] * _sigmoid(cur[:, c:])
    hal = halo_ref[...].astype(F32)
    hglu = hal[:, :c] * _sigmoid(hal[:, c:])
    buf_ref[0:CONV_HALO, :] = jnp.where(pl.program_id(1) > 0, hglu, 0.0)

    first = CONV_HALO - (CONV_KERNEL - 1)
    offsets = [first + j for j in range(CONV_KERNEL)]

    def taps(r, cb):
        cols = slice(cb, cb + LANES)
        acc_ref[r:r + CONV_ROWS, cols] = (
            _shifted_taps(buf_ref, dww_ref, offsets, r, CONV_ROWS, cols) + dwb_ref[:, cols])

    def finish():
        y = acc_ref[...]
        mu = jnp.mean(y, axis=-1, keepdims=True)
        yc = y - mu
        yn = yc * lax.rsqrt(jnp.mean(yc * yc, axis=-1, keepdims=True) + EPS)
        yn = _silu(yn * lng_ref[...] + lnb_ref[...])
        return _dot(yn.astype(BF16), w_ref[...])

    units = [(lambda r=r, cb=cb: taps(r, cb)) for r in range(0, tl, CONV_ROWS) for cb in range(0, c, LANES)]
    return units, finish


def _ssd_part(z_ref, xbc_ref, dt_ref, cw_ref, cb_ref, dtb_ref, a_ref, dexp_ref, ng_ref, w_ref,
              e_ref, tril_ref, xb_ref, xs_ref, bm_ref, cm_ref, y_ref, st_ref):
    tl = z_ref.shape[0]
    q = SSD_CHUNK
    gw = SSD_GROUP_W

    @pl.when(pl.program_id(1) == 0)
    def _():
        st_ref[...] = jnp.zeros_like(st_ref)
        xb_ref[0:SSD_TAIL, :] = jnp.zeros((SSD_TAIL, SSD_XBC), F32)

    xb_ref[SSD_TAIL:SSD_TAIL + tl, :] = xbc_ref[...].astype(F32)
    first = SSD_TAIL - (SSD_CONV - 1)
    offsets = [first + j for j in range(SSD_CONV)]
    for r in range(0, tl, SSD_CONV_ROWS):
        rows = slice(r, r + SSD_CONV_ROWS)
        for cbk in range(0, SSD_XBC, SSD_CONV_COLS):
            cols = slice(cbk, cbk + SSD_CONV_COLS)
            act = _silu(_shifted_taps(xb_ref, cw_ref, offsets, r, SSD_CONV_ROWS, cols) + cb_ref[:, cols])
            if cbk < SSD_INNER:
                xs_ref[rows, cols] = act
            elif cbk < SSD_INNER + SSD_BC:
                bm_ref[rows, cbk - SSD_INNER:cbk - SSD_INNER + SSD_CONV_COLS] = act
            else:
                off = cbk - SSD_INNER - SSD_BC
                cm_ref[rows, off:off + SSD_CONV_COLS] = act
    xb_ref[0:SSD_TAIL, :] = xb_ref[tl:tl + SSD_TAIL, :]

    x = dt_ref[...] + dtb_ref[...]
    dt_all = jnp.maximum(x, 0.0) + jnp.log1p(jnp.exp(-jnp.abs(x)))

    row = lax.broadcasted_iota(jnp.int32, (q, q), 0)
    col = lax.broadcasted_iota(jnp.int32, (q, q), 1)
    tri = row >= col
    lane = lax.broadcasted_iota(jnp.int32, (q, LANES), 1)
    lo_half = lane < SSD_HEAD_DIM
    tril = tril_ref[...]

    ctx = {}

    def chunk_head(r0):
        rows = slice(r0, r0 + q)
        dt_c = dt_all[rows]
        da = dt_c * a_ref[...]
        hi, mid, lo = _split3(da)
        cs = _dot(tril, hi) + _dot(tril, mid) + _dot(tril, lo)
        adj_t = (cs - jnp.log(dt_c)).T
        cs_last = cs[q - 1:q, :]
        w_dec = jnp.exp(cs_last - cs) * dt_c
        dfs = jnp.exp(cs)
        cd = jnp.exp(cs_last)
        cd_hi = cd.astype(BF16)
        cd_lo = (cd - cd_hi.astype(F32)).astype(BF16)
        stack = jnp.concatenate([
            w_dec.astype(BF16), dfs.astype(BF16),
            jnp.broadcast_to(cd_hi, (SUBLANES * 2, LANES)),
            jnp.broadcast_to(cd_lo, (SUBLANES * 2, LANES))], axis=0)
        ex = _dot(stack, e_ref[...])
        w_exp = ex[0:q]
        dfs_exp = ex[q:2 * q]
        cd_exp = ex[2 * q:2 * q + 1] + ex[2 * q + 2 * SUBLANES:2 * q + 2 * SUBLANES + 1]

        ctx[r0] = (cs, adj_t, w_exp, dfs_exp, cd_exp)

    def chunk_body(r0):
        rows = slice(r0, r0 + q)
        cs, adj_t, w_exp, dfs_exp, cd_exp = ctx[r0]
        groups = range(SSD_GROUPS)
        pairs = range(gw // LANES)
        lanes = [slice(g * gw, (g + 1) * gw) for g in groups]
        xs_g = [xs_ref[rows, lanes[g]] for g in groups]
        b_g = [bm_ref[rows, g * SSD_STATE:(g + 1) * SSD_STATE] for g in groups]
        c_g = [cm_ref[rows, g * SSD_STATE:(g + 1) * SSD_STATE].astype(BF16) for g in groups]
        st_g = [st_ref[:, lanes[g]] for g in groups]
        cbm = [_dot_nt(c_g[g], b_g[g].astype(BF16)) for g in groups]
        y_off = [_dot(c_g[g], st_g[g].astype(BF16)) * dfs_exp[:, lanes[g]] for g in groups]
        lhs = {}
        for g in groups:
            for pr in pairs:
                h0 = (g * gw + pr * LANES) // SSD_HEAD_DIM
                sc = []
                for h in (h0, h0 + 1):
                    dec = jnp.where(tri, jnp.exp(cs[:, h:h + 1] - adj_t[h:h + 1, :]), 0.0)
                    sc.append((cbm[g] * dec).astype(BF16))
                lhs[g, pr] = jnp.concatenate(sc, axis=1)
        for g in groups:
            for pr in pairs:
                lanes_p = slice(pr * LANES, (pr + 1) * LANES)
                xp = xs_g[g][:, lanes_p].astype(BF16)
                zero = jnp.zeros_like(xp)
                rhs = jnp.concatenate([jnp.where(lo_half, xp, zero),
                                       jnp.where(lo_half, zero, xp)], axis=0)
                y_ref[rows, g * gw + pr * LANES:g * gw + (pr + 1) * LANES] = (
                    y_off[g][:, lanes_p] + _dot(lhs[g, pr], rhs))
        for g in groups:
            xw = (xs_g[g] * w_exp[:, lanes[g]]).astype(BF16)
            st_ref[:, lanes[g]] = st_g[g] * cd_exp[:, lanes[g]] + _dot(b_g[g].T.astype(BF16), xw)

    def finish():
        y = y_ref[...] + xs_ref[...] * dexp_ref[...]
        y = y * _silu(z_ref[...].astype(F32))
        parts = []
        for g in range(SSD_GROUPS):
            yg = y[:, g * gw:(g + 1) * gw]
            parts.append(yg * lax.rsqrt(jnp.mean(yg * yg, axis=-1, keepdims=True) + EPS))
        yn = jnp.concatenate(parts, axis=1) * ng_ref[...]
        return _dot(yn.astype(BF16), w_ref[...])

    for r0 in range(0, tl, q):
        chunk_head(r0)
    units = [(lambda r0=r0: chunk_body(r0)) for r0 in range(0, tl, q)]
    return units, finish


def _swa_part(sink_ref, q_ref, kc_ref, vc_ref, kp_ref, vp_ref, bias_ref, qg_ref, kg_ref, ek_ref,
              rep_ref, w_ref, att_ref):
    tq = q_ref.shape[0]
    blk = ATTN_BLOCK
    gwid = ATTN_GROUP_W
    inv_d = 1.0 / ATTN_HEAD_DIM

    qf = q_ref[...].astype(F32)
    q_sq = (qf * qf).astype(BF16)
    q_ss = jnp.concatenate([_dot(q_sq[:, g * gwid:(g + 1) * gwid], ek_ref[...])
                            for g in range(ATTN_KV_HEADS)], axis=1)
    qn = (qf * lax.rsqrt(q_ss * inv_d + EPS) * qg_ref[...] * (ATTN_HEAD_DIM ** -0.5)).astype(BF16)

    kf = jnp.concatenate([kp_ref[...], kc_ref[...]], axis=0).astype(F32)
    k_ss = _dot((kf * kf).astype(BF16), ek_ref[...])
    kn = (kf * lax.rsqrt(k_ss * inv_d + EPS) * kg_ref[...]).astype(BF16)
    k_rep = _dot(kn, rep_ref[...]).astype(BF16)
    v_all = jnp.concatenate([vp_ref[...], vc_ref[...]], axis=0)
    v_rep = _dot(v_all, rep_ref[...]).astype(BF16)

    lane = lax.broadcasted_iota(jnp.int32, (1, gwid), 1) // ATTN_HEAD_DIM
    key_col = lax.broadcasted_iota(jnp.int32, (blk, 2 * blk), 1)
    first_tile = pl.program_id(1) == 0

    cur_side = (lax.broadcasted_iota(jnp.int32, (blk, blk), 1)
                <= lax.broadcasted_iota(jnp.int32, (blk, blk), 0))

    def block(n):
        r0 = n * blk
        folded = []
        for g in range(ATTN_KV_HEADS):
            lanes_g = slice(g * gwid, (g + 1) * gwid)
            q_g = qn[r0:r0 + blk, lanes_g]
            k_g = k_rep[r0:r0 + 2 * blk, lanes_g]
            for hh in range(ATTN_REP):
                h = g * ATTN_REP + hh
                s = _dot_nt(jnp.where(lane == hh, q_g, jnp.zeros_like(q_g)), k_g) + bias_ref[h]
                if n == 0:
                    s = jnp.where(jnp.logical_and(first_tile, key_col < blk), NEG_INF, s)
                folded.append(jnp.maximum(s[:, :blk], s[:, blk:]))
        probs = []
        for h in range(ATTN_HEADS):
            sink = sink_ref[h]
            m = jnp.maximum(jnp.max(folded[h], axis=-1, keepdims=True), sink)
            pe = jnp.exp(folded[h] - m)
            den = jnp.sum(pe, axis=-1, keepdims=True) + jnp.exp(sink - m)
            pr = pe * (1.0 / den)
            probs.append(jnp.concatenate([jnp.where(cur_side, 0.0, pr), jnp.where(cur_side, pr, 0.0)],
                                         axis=1).astype(BF16))
        for g in range(ATTN_KV_HEADS):
            lanes_g = slice(g * gwid, (g + 1) * gwid)
            v_g = v_rep[r0:r0 + 2 * blk, lanes_g]
            v_blocks = [jnp.where(lane == hh, v_g, jnp.zeros_like(v_g)) for hh in range(ATTN_REP)]
            att_ref[r0:r0 + blk, lanes_g] = _dot(
                jnp.concatenate(probs[g * ATTN_REP:(g + 1) * ATTN_REP], axis=1),
                jnp.concatenate(v_blocks, axis=0))

    def finish():
        return _dot(att_ref[...].astype(BF16), w_ref[...])

    units = [(lambda n=n: block(n)) for n in range(tq // blk)]
    return units, finish


def _mixers_kernel(sink_ref, p_ref, halo_ref, kvp_ref, dt_ref, gb_ref,
                   dww_ref, dwb_ref, lng_ref, lnb_ref, wa_ref,
                   cw_ref, cb_ref, dtb_ref, a_ref, dexp_ref, ng_ref, wb_ref, e_ref, tril_ref,
                   bias_ref, qg_ref, kg_ref, ek_ref, rep_ref, wc_ref,
                   o_ref,
                   cbuf_ref, cacc_ref, xb_ref, xs_ref, bm_ref, cm_ref, y_ref, st_ref, att_ref):
    seg = lambda off, width: p_ref.at[:, off:off + width]
    gates_ref = seg(P_GATE, N_BRANCH * D_MODEL)
    cur_ref = seg(P_CONV, 2 * CONV_DIM)
    z_ref = seg(P_Z, SSD_INNER)
    xbc_ref = seg(P_XBC, SSD_XBC)
    q_ref = seg(P_Q, ATTN_DIM)
    kc_ref = seg(P_K, ATTN_KV_DIM)
    vc_ref = seg(P_V, ATTN_KV_DIM)
    kp_ref = kvp_ref.at[:, 0:ATTN_KV_DIM]
    vp_ref = kvp_ref.at[:, ATTN_KV_DIM:2 * ATTN_KV_DIM]
    conv_units, conv_finish = _conv_part(cur_ref, halo_ref, dww_ref, dwb_ref, lng_ref, lnb_ref, wa_ref,
                                         cbuf_ref, cacc_ref)
    ssd_units, ssd_finish = _ssd_part(z_ref, xbc_ref, dt_ref, cw_ref, cb_ref, dtb_ref, a_ref, dexp_ref,
                                      ng_ref, wb_ref, e_ref, tril_ref, xb_ref, xs_ref, bm_ref, cm_ref,
                                      y_ref, st_ref)
    swa_units, swa_finish = _swa_part(sink_ref, q_ref, kc_ref, vc_ref, kp_ref, vp_ref, bias_ref, qg_ref,
                                      kg_ref, ek_ref, rep_ref, wc_ref, att_ref)
    for unit in _round_robin(ssd_units, swa_units):
        unit()
    y_c = _gate(gates_ref, gb_ref, 2) * swa_finish()
    y_b = _gate(gates_ref, gb_ref, 1) * ssd_finish()
    for unit in conv_units:
        unit()
    y_a = _gate(gates_ref, gb_ref, 0) * conv_finish()
    o_ref[...] = (y_a + y_b + y_c).astype(BF16)


def _mixers(p, dt, sinks, gb, conv_consts, ssd_consts, swa_consts, batch, seq, tl):
    nl = seq // tl
    cw = 2 * CONV_DIM
    kvw = 2 * ATTN_KV_DIM
    assert P_V == P_K + ATTN_KV_DIM and P_K % kvw == 0 and P_CONV % cw == 0

    def prev_rows(rows, width, col_off):
        per_tile = tl // rows
        return pl.BlockSpec(
            (rows, width), lambda b, l: (jnp.maximum((b * nl + l) * per_tile - 1, 0), col_off // width))

    consts = (gb,) + tuple(conv_consts) + tuple(ssd_consts) + tuple(swa_consts)
    in_specs = (
        [pl.BlockSpec(memory_space=pltpu.SMEM),
         pl.BlockSpec((tl, P_COLS), lambda b, l: (b * nl + l, 0)),
         prev_rows(CONV_HALO, cw, P_CONV),
         prev_rows(ATTN_BLOCK, kvw, P_K),
         pl.BlockSpec((tl, DT_PAD), lambda b, l: (b * nl + l, 0))]
        + [_const_spec(a.shape) for a in consts])
    return pl.pallas_call(
        _mixers_kernel,
        grid=(batch, nl),
        in_specs=in_specs,
        out_specs=pl.BlockSpec((tl, D_MODEL), lambda b, l: (b * nl + l, 0)),
        out_shape=jax.ShapeDtypeStruct((batch * seq, D_MODEL), BF16),
        scratch_shapes=[
            pltpu.VMEM((CONV_HALO + tl, CONV_DIM), F32),
            pltpu.VMEM((tl, CONV_DIM), F32),
            pltpu.VMEM((SSD_TAIL + tl, SSD_XBC), F32),
            pltpu.VMEM((tl, SSD_INNER), F32),
            pltpu.VMEM((tl, SSD_BC), F32),
            pltpu.VMEM((tl, SSD_BC), F32),
            pltpu.VMEM((tl, SSD_INNER), F32),
            pltpu.VMEM((SSD_STATE, SSD_INNER), F32),
            pltpu.VMEM((tl, ATTN_DIM), F32),
        ],
        compiler_params=_cparams(("parallel", "arbitrary")),
        name="mixers",
    )(sinks, p, p, p, dt, *consts)


def _band_buckets():
    qi = np.arange(ATTN_BLOCK)[:, None] + ATTN_BLOCK
    kj = np.arange(2 * ATTN_BLOCK)[None, :]
    dist = qi - kj
    max_exact = REL_BUCKETS // 2
    d = np.maximum(dist, 1).astype(np.float32)
    large = max_exact + (np.log(d / np.float32(max_exact)) / np.float32(math.log(REL_MAX_DIST / max_exact))
                         * np.float32(REL_BUCKETS - max_exact)).astype(np.int32)
    large = np.minimum(large, REL_BUCKETS - 1)
    bucket = np.where(dist < max_exact, np.maximum(dist, 0), large).astype(np.int32)
    in_window = (dist >= 0) & (dist < ATTN_WINDOW)
    return bucket, in_window.astype(np.int32)


def _bias_kernel(tab_ref, bucket_ref, win_ref, o_ref):
    bucket = bucket_ref[...]
    win = win_ref[...] > 0
    for h in range(ATTN_HEADS):
        acc = jnp.zeros(bucket.shape, F32)
        for b in range(REL_BUCKETS):
            acc = jnp.where(bucket == b, tab_ref[b, h], acc)
        o_ref[h] = jnp.where(win, acc, NEG_INF)


def _band_bias(rel_table):
    bucket, win = _band_buckets()
    shape = (ATTN_BLOCK, 2 * ATTN_BLOCK)
    return pl.pallas_call(
        _bias_kernel,
        in_specs=[pl.BlockSpec(memory_space=pltpu.SMEM),
                  pl.BlockSpec(shape, lambda: (0, 0)),
                  pl.BlockSpec(shape, lambda: (0, 0))],
        out_specs=pl.BlockSpec((ATTN_HEADS,) + shape, lambda: (0, 0, 0)),
        out_shape=jax.ShapeDtypeStruct((ATTN_HEADS,) + shape, F32),
        name="band_bias",
    )(rel_table, jnp.asarray(bucket), jnp.asarray(win))


def _memkv_kernel(mem_ref, g_ref, w_ref, kg_ref, k_ref, v_ref):
    mn = _rms(mem_ref[0], g_ref[0]).astype(BF16)
    kv = _dot(mn, w_ref[0])
    parts = []
    for h in range(XATTN_HEADS):
        kh = kv[:, h * XATTN_HEAD_DIM:(h + 1) * XATTN_HEAD_DIM]
        parts.append(_rms(kh, kg_ref[0]))
    k_ref[0, 0] = jnp.concatenate(parts, axis=1).astype(BF16)
    v_ref[0, 0] = kv[:, D_MODEL:].astype(BF16)


def _mem_kv(mem, g, w, kg):
    depth = w.shape[0]
    batch = mem.shape[0]
    out = jax.ShapeDtypeStruct((depth, batch, MEM_LEN, D_MODEL), BF16)
    return pl.pallas_call(
        _memkv_kernel,
        grid=(depth, batch),
        in_specs=[
            pl.BlockSpec((1, MEM_LEN, D_MODEL), lambda i, b: (b, 0, 0)),
            pl.BlockSpec((1, 1, D_MODEL), lambda i, b: (i, 0, 0)),
            pl.BlockSpec((1, D_MODEL, 2 * D_MODEL), lambda i, b: (i, 0, 0)),
            pl.BlockSpec((1, 1, XATTN_HEAD_DIM), lambda i, b: (i, 0, 0)),
        ],
        out_specs=[pl.BlockSpec((1, 1, MEM_LEN, D_MODEL), lambda i, b: (i, b, 0, 0)),
                   pl.BlockSpec((1, 1, MEM_LEN, D_MODEL), lambda i, b: (i, b, 0, 0))],
        out_shape=[out, out],
        compiler_params=_cparams(("arbitrary", "arbitrary")),
        name="mem_kv",
    )(mem, g, w, kg)


def _tail_body(h_ref, m_ref, k_ref, v_ref, wmix_ref, gx_ref, wq_ref, qg_ref, wo_ref, gm_ref,
               wup_ref, wdn_ref, att_ref):
    h = h_ref[...] + _dot(m_ref[...], wmix_ref[...])

    qx = _dot(_rms(h, gx_ref[...]).astype(BF16), wq_ref[...])
    lanes = [slice(hd * XATTN_HEAD_DIM, (hd + 1) * XATTN_HEAD_DIM) for hd in range(XATTN_HEADS)]
    scores = [_dot_nt((_rms(qx[:, ln], qg_ref[...]) * (XATTN_HEAD_DIM ** -0.5)).astype(BF16),
                      k_ref[0, 0, :, ln]) for ln in lanes]
    probs = []
    for s in scores:
        pe = jnp.exp(s - jnp.max(s, axis=-1, keepdims=True))
        probs.append((pe * (1.0 / jnp.sum(pe, axis=-1, keepdims=True))).astype(BF16))
    for ln, pr in zip(lanes, probs):
        att_ref[:, ln] = _dot(pr, v_ref[0, 0, :, ln])
    h = h + _dot(att_ref[...].astype(BF16), wo_ref[...])

    u = _rms(h, gm_ref[...]).astype(BF16)
    acc = h
    for cb in range(0, MLP_HIDDEN, D_MODEL):
        up = jnp.maximum(_dot(u, wup_ref[:, cb:cb + D_MODEL]), 0.0)
        acc = acc + _dot((up * up).astype(BF16), wdn_ref[cb:cb + D_MODEL, :])
    return acc


def _tail_kernel(h_ref, m_ref, k_ref, v_ref, wmix_ref, gx_ref, wq_ref, qg_ref, wo_ref, gm_ref,
                 wup_ref, wdn_ref, gn_ref, wdt_ref, o_ref, u_ref, dt_ref, att_ref):
    h = _tail_body(h_ref, m_ref, k_ref, v_ref, wmix_ref, gx_ref, wq_ref, qg_ref, wo_ref, gm_ref,
                   wup_ref, wdn_ref, att_ref)
    o_ref[...] = h
    u = _rms(h, gn_ref[...]).astype(BF16)
    u_ref[...] = u
    dt_ref[...] = _dot(u, wdt_ref[...])


def _tail_last_kernel(h_ref, m_ref, k_ref, v_ref, wmix_ref, gx_ref, wq_ref, qg_ref, wo_ref, gm_ref,
                      wup_ref, wdn_ref, o_ref, att_ref):
    o_ref[...] = _tail_body(h_ref, m_ref, k_ref, v_ref, wmix_ref, gx_ref, wq_ref, qg_ref, wo_ref,
                            gm_ref, wup_ref, wdn_ref, att_ref)


def _tail(h2d, merged, k_mem, v_mem, layer, consts, next_consts, batch, seq, tl):
    nl = seq // tl
    t = batch * seq
    act = lambda b, l: (b * nl + l, 0)
    kv_spec = pl.BlockSpec((1, 1, MEM_LEN, D_MODEL), lambda b, l: (layer, b, 0, 0))
    in_specs = ([pl.BlockSpec((tl, D_MODEL), act), pl.BlockSpec((tl, D_MODEL), act), kv_spec, kv_spec]
                + [_const_spec(a.shape) for a in consts])
    out_specs = [pl.BlockSpec((tl, D_MODEL), act)]
    out_shape = [jax.ShapeDtypeStruct((t, D_MODEL), F32)]
    if next_consts is None:
        body, extra = _tail_last_kernel, ()
    else:
        body, extra = _tail_kernel, tuple(next_consts)
        in_specs += [_const_spec(a.shape) for a in extra]
        out_specs += [pl.BlockSpec((tl, D_MODEL), act), pl.BlockSpec((tl, DT_PAD), act)]
        out_shape += [jax.ShapeDtypeStruct((t, D_MODEL), BF16), jax.ShapeDtypeStruct((t, DT_PAD), F32)]
    return pl.pallas_call(
        body,
        grid=(batch, nl),
        in_specs=in_specs,
        out_specs=out_specs,
        out_shape=out_shape,
        scratch_shapes=[pltpu.VMEM((tl, D_MODEL), F32)],
        compiler_params=_cparams(("parallel", "arbitrary")),
        name="tail",
    )(h2d, merged, k_mem, v_mem, *consts, *extra)


def _head_indicator(width, head_dim):
    idx = np.arange(width) // head_dim
    return (idx[:, None] == idx[None, :]).astype(np.float32)


def _tile_sizes(seq):
    pick = lambda cands: next(c for c in cands if seq % c == 0)
    return dict(tm=pick((1024, 512, 256, 128)), tl_mix=pick((256, 128)), tl_tail=pick((512, 256, 128)))


def kernel(x, mem, rel_table, norm_mix, w_in, gate_bias, conv_dw_w, conv_dw_b, conv_ln_g, conv_ln_b, w_conv_out, ssd_conv_w, ssd_conv_b, ssd_dt_bias, ssd_A_log, ssd_D, ssd_norm_g, w_ssd_out, attn_q_norm, attn_k_norm, attn_sinks, w_attn_out, w_mix_out, norm_xattn, norm_mem, w_xq, w_xkv, xattn_q_norm, xattn_k_norm, w_xo, norm_mlp, w_mlp_up, w_mlp_down):
    batch, seq, _ = x.shape
    depth = w_in.shape[0]
    ts = _tile_sizes(seq)
    assert seq % ATTN_BLOCK == 0 and seq % SSD_CHUNK == 0

    def w_p(i):
        w = w_in[i]
        return jnp.concatenate([w[:, OFF_GATE:IN_COLS], w[:, OFF_XBC:OFF_DT], w[:, OFF_CONV:OFF_Z],
                                w[:, OFF_Z:OFF_XBC], w[:, OFF_Q:OFF_GATE]], axis=1).astype(BF16)

    def w_dt(i):
        return jnp.pad(w_in[i][:, OFF_DT:OFF_Q], ((0, 0), (0, DT_PAD - SSD_HEADS))).astype(BF16)

    pad_h = ((0, 0), (0, DT_PAD - SSD_HEADS))
    dt_bias = jnp.pad(ssd_dt_bias, pad_h)[:, None, :]
    a_row = -jnp.exp(jnp.pad(ssd_A_log, pad_h))[:, None, :]
    d_exp = jnp.repeat(ssd_D, SSD_HEAD_DIM, axis=1)[:, None, :]

    e_np = np.zeros((DT_PAD, SSD_INNER), np.float32)
    e_np[np.arange(SSD_INNER) // SSD_HEAD_DIM, np.arange(SSD_INNER)] = 1.0
    e_mat = jnp.asarray(e_np, BF16)
    tril = jnp.asarray(np.tril(np.ones((SSD_CHUNK, SSD_CHUNK), np.float32)), BF16)
    ek = jnp.asarray(_head_indicator(ATTN_KV_DIM, ATTN_HEAD_DIM), BF16)
    rep_np = np.zeros((ATTN_KV_DIM, ATTN_DIM), np.float32)
    lanes = np.arange(ATTN_DIM)
    rep_np[(lanes // ATTN_GROUP_W) * ATTN_HEAD_DIM + lanes % ATTN_HEAD_DIM, lanes] = 1.0
    rep = jnp.asarray(rep_np, BF16)

    bias = _band_bias(rel_table)
    k_mem, v_mem = _mem_kv(mem, norm_mem[:, None, :], w_xkv.astype(BF16), xattn_k_norm[:, None, :])

    row = lambda a, i: a[i][None, :]
    h = x.reshape(batch * seq, D_MODEL)
    u, dt = _norm_dt(h, row(norm_mix, 0), w_dt(0), ts["tm"])
    for i in range(depth):
        p = _inproj(u, w_p(i), ts["tm"], P_COLS // 4)
        conv_consts = (conv_dw_w[i], row(conv_dw_b, i), row(conv_ln_g, i), row(conv_ln_b, i),
                       w_conv_out[i].astype(BF16))
        ssd_consts = (ssd_conv_w[i], row(ssd_conv_b, i), dt_bias[i], a_row[i], d_exp[i],
                      row(ssd_norm_g, i), w_ssd_out[i].astype(BF16), e_mat, tril)
        swa_consts = (bias, jnp.tile(attn_q_norm[i], ATTN_HEADS)[None, :],
                      jnp.tile(attn_k_norm[i], ATTN_KV_HEADS)[None, :], ek, rep,
                      w_attn_out[i].astype(BF16))
        merged = _mixers(p, dt, attn_sinks[i], gate_bias[i], conv_consts, ssd_consts, swa_consts,
                         batch, seq, ts["tl_mix"])
        tail_consts = (w_mix_out[i].astype(BF16), row(norm_xattn, i), w_xq[i].astype(BF16),
                       row(xattn_q_norm, i), w_xo[i].astype(BF16), row(norm_mlp, i),
                       w_mlp_up[i].astype(BF16), w_mlp_down[i].astype(BF16))
        if i + 1 < depth:
            h, u, dt = _tail(h, merged, k_mem, v_mem, i, tail_consts, (row(norm_mix, i + 1), w_dt(i + 1)),
                             batch, seq, ts["tl_tail"])
        else:
            (h,) = _tail(h, merged, k_mem, v_mem, i, tail_consts, None, batch, seq, ts["tl_tail"])
    return h.reshape(batch, seq, D_MODEL)
```

```python
import math

import numpy as np
import jax
import jax.numpy as jnp
from jax import lax
from jax.experimental import pallas as pl
from jax.experimental.pallas import tpu as pltpu

F32 = jnp.float32
BF16 = jnp.bfloat16

D_MODEL = 1024
MEM_LEN = 256
EPS = 1e-6
NEG_INF = -1e30

CONV_DIM = D_MODEL
CONV_KERNEL = 31

SSD_INNER = 2 * D_MODEL
SSD_HEAD_DIM = 64
SSD_HEADS = SSD_INNER // SSD_HEAD_DIM
SSD_GROUPS = 4
SSD_STATE = 128
SSD_CONV = 4
SSD_CHUNK = 128
SSD_BC = SSD_GROUPS * SSD_STATE
SSD_XBC = SSD_INNER + 2 * SSD_BC
SSD_GROUP_W = SSD_INNER // SSD_GROUPS

ATTN_HEADS = 16
ATTN_KV_HEADS = 4
ATTN_HEAD_DIM = 64
ATTN_DIM = ATTN_HEADS * ATTN_HEAD_DIM
ATTN_KV_DIM = ATTN_KV_HEADS * ATTN_HEAD_DIM
ATTN_REP = ATTN_HEADS // ATTN_KV_HEADS
ATTN_WINDOW = 128
ATTN_BLOCK = 128
ATTN_GROUP_W = ATTN_REP * ATTN_HEAD_DIM

REL_BUCKETS = 32
REL_MAX_DIST = 128

XATTN_HEADS = 4
XATTN_HEAD_DIM = D_MODEL // XATTN_HEADS

N_BRANCH = 3
MLP_HIDDEN = 4 * D_MODEL

OFF_CONV = 0
OFF_Z = OFF_CONV + 2 * CONV_DIM
OFF_XBC = OFF_Z + SSD_INNER
OFF_DT = OFF_XBC + SSD_XBC
OFF_Q = OFF_DT + SSD_HEADS
OFF_K = OFF_Q + ATTN_DIM
OFF_V = OFF_K + ATTN_KV_DIM
OFF_GATE = OFF_V + ATTN_KV_DIM
IN_COLS = OFF_GATE + N_BRANCH * D_MODEL

LANES = 128
SUBLANES = 8

P_GATE = 0
P_XBC = P_GATE + N_BRANCH * D_MODEL
P_CONV = P_XBC + SSD_XBC
P_Z = P_CONV + 2 * CONV_DIM
P_Q = P_Z + SSD_INNER
P_K = P_Q + ATTN_DIM
P_V = P_K + ATTN_KV_DIM
P_COLS = P_V + ATTN_KV_DIM
DT_PAD = LANES

VMEM_LIMIT = 56 * 1024 * 1024


def _cparams(semantics):
    return pltpu.CompilerParams(dimension_semantics=semantics, vmem_limit_bytes=VMEM_LIMIT)


def _const_spec(shape):
    nd = len(shape)
    return pl.BlockSpec(shape, lambda *_: (0,) * nd, pipeline_mode=pl.Buffered(1))


NEG_LOG2_E = -1.4426950408889634


def _sigmoid(x):
    return 1.0 / (1.0 + jnp.exp2(x * NEG_LOG2_E))


def _silu(x):
    return x * _sigmoid(x)


def _rms(x, g):
    return x * lax.rsqrt(jnp.mean(x * x, axis=-1, keepdims=True) + EPS) * g


def _dot(a, b):
    return jnp.dot(a, b, preferred_element_type=F32)


def _dot_nt(a, b):
    return lax.dot_general(a, b, (((1,), (1,)), ((), ())), preferred_element_type=F32)


def _shifted_taps(buf_ref, w_ref, offsets, r, rows, cols):
    acc = None
    for s in range(SUBLANES):
        group = [(j, o) for j, o in enumerate(offsets) if o % SUBLANES == s]
        if not group:
            continue
        n = rows if s == 0 else rows + SUBLANES
        t_s = None
        for j, o in group:
            base = r + o - s
            term = w_ref[j:j + 1, cols] * buf_ref[base:base + n, cols]
            t_s = term if t_s is None else t_s + term
        part = t_s if s == 0 else _shift_rows(t_s, s, rows)
        acc = part if acc is None else acc + part
    return acc


def _shift_rows(t, s, rows):
    tiles = [t[i:i + SUBLANES] for i in range(0, rows + SUBLANES, SUBLANES)]
    upper = lax.broadcasted_iota(jnp.int32, tiles[0].shape, 0) >= s
    return jnp.concatenate([pltpu.roll(jnp.where(upper, a, b), SUBLANES - s, axis=0)
                            for a, b in zip(tiles[:-1], tiles[1:])], axis=0)


def _split3(x):
    hi = x.astype(BF16)
    r1 = x - hi.astype(F32)
    mid = r1.astype(BF16)
    lo = (r1 - mid.astype(F32)).astype(BF16)
    return hi, mid, lo


def _norm_dt_kernel(x_ref, g_ref, wdt_ref, u_ref, dt_ref):
    u = _rms(x_ref[...], g_ref[...]).astype(BF16)
    u_ref[...] = u
    dt_ref[...] = _dot(u, wdt_ref[...])


def _norm_dt(h2d, g, wdt, tm):
    t = h2d.shape[0]
    return pl.pallas_call(
        _norm_dt_kernel,
        grid=(t // tm,),
        in_specs=[pl.BlockSpec((tm, D_MODEL), lambda i: (i, 0)),
                  _const_spec((1, D_MODEL)),
                  _const_spec((D_MODEL, DT_PAD))],
        out_specs=[pl.BlockSpec((tm, D_MODEL), lambda i: (i, 0)),
                   pl.BlockSpec((tm, DT_PAD), lambda i: (i, 0))],
        out_shape=[jax.ShapeDtypeStruct((t, D_MODEL), BF16),
                   jax.ShapeDtypeStruct((t, DT_PAD), F32)],
        compiler_params=_cparams(("parallel",)),
        name="norm_dt",
    )(h2d, g, wdt)


W_SEGMENTS = ((P_GATE, OFF_GATE, N_BRANCH * D_MODEL), (P_XBC, OFF_XBC, SSD_XBC),
              (P_CONV, OFF_CONV, 2 * CONV_DIM), (P_Z, OFF_Z, SSD_INNER),
              (P_Q, OFF_Q, ATTN_DIM + 2 * ATTN_KV_DIM))
REGROUP_ROWS = 128


def _lane_window(w_ref, start, width):
    lo = start // LANES * LANES
    hi = -(-(start + width) // LANES) * LANES
    return w_ref[0, :, lo:hi][:, start - lo:start - lo + width]


def _regroup_kernel(w_ref, o_ref, dt_ref):
    for dst, src, width in W_SEGMENTS:
        for c in range(0, width, LANES * 4):
            n = min(LANES * 4, width - c)
            o_ref[0, :, dst + c:dst + c + n] = _lane_window(w_ref, src + c, n).astype(BF16)
    dt = _lane_window(w_ref, OFF_DT, SSD_HEADS)
    dt_ref[0] = jnp.concatenate(
        [dt, jnp.zeros((dt.shape[0], DT_PAD - SSD_HEADS), F32)], axis=1).astype(BF16)


def _regroup_weights(w_in):
    depth, rows, cols = w_in.shape
    return pl.pallas_call(
        _regroup_kernel,
        grid=(depth, rows // REGROUP_ROWS),
        in_specs=[pl.BlockSpec((1, REGROUP_ROWS, cols), lambda i, r: (i, r, 0))],
        out_specs=[pl.BlockSpec((1, REGROUP_ROWS, P_COLS), lambda i, r: (i, r, 0)),
                   pl.BlockSpec((1, REGROUP_ROWS, DT_PAD), lambda i, r: (i, r, 0))],
        out_shape=[jax.ShapeDtypeStruct((depth, rows, P_COLS), BF16),
                   jax.ShapeDtypeStruct((depth, rows, DT_PAD), BF16)],
        compiler_params=_cparams(("parallel", "parallel")),
        name="regroup_weights",
    )(w_in)


def _inproj_kernel(u_ref, w_ref, p_ref):
    p_ref[...] = _dot(u_ref[...], w_ref[...]).astype(BF16)


def _inproj(u, w_all, layer, tm, tn):
    t = u.shape[0]
    return pl.pallas_call(
        _inproj_kernel,
        grid=(t // tm, P_COLS // tn),
        in_specs=[pl.BlockSpec((tm, D_MODEL), lambda i, j: (i, 0)),
                  pl.BlockSpec((None, D_MODEL, tn), lambda i, j: (layer, 0, j))],
        out_specs=pl.BlockSpec((tm, tn), lambda i, j: (i, j)),
        out_shape=jax.ShapeDtypeStruct((t, P_COLS), BF16),
        compiler_params=_cparams(("parallel", "arbitrary")),
        name="inproj",
    )(u, w_all)


CONV_HALO = 32
CONV_ROWS = 128
SSD_TAIL = SUBLANES
SSD_CONV_ROWS = 64
SSD_CONV_COLS = 256


def _round_robin(*streams):
    total = max(len(s) for s in streams)
    order = []
    for step in range(total):
        for s in streams:
            lo = step * len(s) // total
            hi = (step + 1) * len(s) // total
            order.extend(s[lo:hi])
    return order


def _gate(gates_ref, gb_ref, branch):
    cols = slice(branch * D_MODEL, (branch + 1) * D_MODEL)
    return _sigmoid(gates_ref[:, cols].astype(F32) + gb_ref[branch:branch + 1, :])


def _conv_part(cur_ref, halo_ref, dww_ref, dwb_ref, lng_ref, lnb_ref, w_ref, buf_ref, acc_ref):
    tl = cur_ref.shape[0]
    c = CONV_DIM
    cur = cur_ref[...].astype(F32)
    buf_ref[CONV_HALO:CONV_HALO + tl, :] = cur[:, :c] * _sigmoid(cur[:, c:])
    hal = halo_ref[...].astype(F32)
    hglu = hal[:, :c] * _sigmoid(hal[:, c:])
    buf_ref[0:CONV_HALO, :] = jnp.where(pl.program_id(1) > 0, hglu, 0.0)

    first = CONV_HALO - (CONV_KERNEL - 1)
    offsets = [first + j for j in range(CONV_KERNEL)]

    def taps(r, cb):
        cols = slice(cb, cb + LANES)
        acc_ref[r:r + CONV_ROWS, cols] = (
            _shifted_taps(buf_ref, dww_ref, offsets, r, CONV_ROWS, cols) + dwb_ref[:, cols])

    def finish():
        y = acc_ref[...]
        mu = jnp.mean(y, axis=-1, keepdims=True)
        yc = y - mu
        yn = yc * lax.rsqrt(jnp.mean(yc * yc, axis=-1, keepdims=True) + EPS)
        yn = _silu(yn * lng_ref[...] + lnb_ref[...])
        return _dot(yn.astype(BF16), w_ref[...])

    units = [(lambda r=r, cb=cb: taps(r, cb)) for r in range(0, tl, CONV_ROWS) for cb in range(0, c, LANES)]
    return units, finish


def _ssd_part(z_ref, xbc_ref, dt_ref, cw_ref, cb_ref, dtb_ref, a_ref, dexp_ref, ng_ref, w_ref,
              e_ref, tril_ref, xb_ref, xs_ref, bm_ref, cm_ref, y_ref, st_ref):
    tl = z_ref.shape[0]
    q = SSD_CHUNK
    gw = SSD_GROUP_W

    @pl.when(pl.program_id(1) == 0)
    def _():
        st_ref[...] = jnp.zeros_like(st_ref)
        xb_ref[0:SSD_TAIL, :] = jnp.zeros((SSD_TAIL, SSD_XBC), F32)

    xb_ref[SSD_TAIL:SSD_TAIL + tl, :] = xbc_ref[...].astype(F32)
    first = SSD_TAIL - (SSD_CONV - 1)
    offsets = [first + j for j in range(SSD_CONV)]
    for r in range(0, tl, SSD_CONV_ROWS):
        rows = slice(r, r + SSD_CONV_ROWS)
        for cbk in range(0, SSD_XBC, SSD_CONV_COLS):
            cols = slice(cbk, cbk + SSD_CONV_COLS)
            act = _silu(_shifted_taps(xb_ref, cw_ref, offsets, r, SSD_CONV_ROWS, cols) + cb_ref[:, cols])
            if cbk < SSD_INNER:
                xs_ref[rows, cols] = act
            elif cbk < SSD_INNER + SSD_BC:
                bm_ref[rows, cbk - SSD_INNER:cbk - SSD_INNER + SSD_CONV_COLS] = act
            else:
                off = cbk - SSD_INNER - SSD_BC
                cm_ref[rows, off:off + SSD_CONV_COLS] = act
    xb_ref[0:SSD_TAIL, :] = xb_ref[tl:tl + SSD_TAIL, :]

    x = dt_ref[...] + dtb_ref[...]
    dt_all = jnp.maximum(x, 0.0) + jnp.log1p(jnp.exp(-jnp.abs(x)))

    row = lax.broadcasted_iota(jnp.int32, (q, q), 0)
    col = lax.broadcasted_iota(jnp.int32, (q, q), 1)
    tri = row >= col
    lane = lax.broadcasted_iota(jnp.int32, (q, LANES), 1)
    lo_half = lane < SSD_HEAD_DIM
    tril = tril_ref[...]

    ctx = {}

    def chunk_head(r0):
        rows = slice(r0, r0 + q)
        dt_c = dt_all[rows]
        da = dt_c * a_ref[...]
        hi, mid, lo = _split3(da)
        cs = _dot(tril, hi) + _dot(tril, mid) + _dot(tril, lo)
        adj_t = (cs - jnp.log(dt_c)).T
        cs_last = cs[q - 1:q, :]
        w_dec = jnp.exp(cs_last - cs) * dt_c
        dfs = jnp.exp(cs)
        cd = jnp.exp(cs_last)
        cd_hi = cd.astype(BF16)
        cd_lo = (cd - cd_hi.astype(F32)).astype(BF16)
        stack = jnp.concatenate([
            w_dec.astype(BF16), dfs.astype(BF16),
            jnp.broadcast_to(cd_hi, (SUBLANES * 2, LANES)),
            jnp.broadcast_to(cd_lo, (SUBLANES * 2, LANES))], axis=0)
        ex = _dot(stack, e_ref[...])
        w_exp = ex[0:q]
        dfs_exp = ex[q:2 * q]
        cd_exp = ex[2 * q:2 * q + 1] + ex[2 * q + 2 * SUBLANES:2 * q + 2 * SUBLANES + 1]

        ctx[r0] = (cs, adj_t, w_exp, dfs_exp, cd_exp)

    def chunk_body(r0):
        rows = slice(r0, r0 + q)
        cs, adj_t, w_exp, dfs_exp, cd_exp = ctx[r0]
        groups = range(SSD_GROUPS)
        pairs = range(gw // LANES)
        lanes = [slice(g * gw, (g + 1) * gw) for g in groups]
        xs_g = [xs_ref[rows, lanes[g]] for g in groups]
        b_g = [bm_ref[rows, g * SSD_STATE:(g + 1) * SSD_STATE] for g in groups]
        c_g = [cm_ref[rows, g * SSD_STATE:(g + 1) * SSD_STATE].astype(BF16) for g in groups]
        st_g = [st_ref[:, lanes[g]] for g in groups]
        cbm = [_dot_nt(c_g[g], b_g[g].astype(BF16)) for g in groups]
        y_off = [_dot(c_g[g], st_g[g].astype(BF16)) * dfs_exp[:, lanes[g]] for g in groups]
        lhs = {}
        for g in groups:
            for pr in pairs:
                h0 = (g * gw + pr * LANES) // SSD_HEAD_DIM
                sc = []
                for h in (h0, h0 + 1):
                    dec = jnp.where(tri, jnp.exp(cs[:, h:h + 1] - adj_t[h:h + 1, :]), 0.0)
                    sc.append((cbm[g] * dec).astype(BF16))
                lhs[g, pr] = jnp.concatenate(sc, axis=1)
        for g in groups:
            for pr in pairs:
                lanes_p = slice(pr * LANES, (pr + 1) * LANES)
                xp = xs_g[g][:, lanes_p].astype(BF16)
                zero = jnp.zeros_like(xp)
                rhs = jnp.concatenate([jnp.where(lo_half, xp, zero),
                                       jnp.where(lo_half, zero, xp)], axis=0)
                y_ref[rows, g * gw + pr * LANES:g * gw + (pr + 1) * LANES] = (
                    y_off[g][:, lanes_p] + _dot(lhs[g, pr], rhs))
        for g in groups:
            xw = (xs_g[g] * w_exp[:, lanes[g]]).astype(BF16)
            st_ref[:, lanes[g]] = st_g[g] * cd_exp[:, lanes[g]] + _dot(b_g[g].T.astype(BF16), xw)

    def finish():
        y = y_ref[...] + xs_ref[...] * dexp_ref[...]
        y = y * _silu(z_ref[...].astype(F32))
        parts = []
        for g in range(SSD_GROUPS):
            yg = y[:, g * gw:(g + 1) * gw]
            parts.append(yg * lax.rsqrt(jnp.mean(yg * yg, axis=-1, keepdims=True) + EPS))
        yn = jnp.concatenate(parts, axis=1) * ng_ref[...]
        return _dot(yn.astype(BF16), w_ref[...])

    for r0 in range(0, tl, q):
        chunk_head(r0)
    units = [(lambda r0=r0: chunk_body(r0)) for r0 in range(0, tl, q)]
    return units, finish


def _swa_part(sink_ref, q_ref, kc_ref, vc_ref, kp_ref, vp_ref, bias_ref, qg_ref, kg_ref, ek_ref,
              rep_ref, w_ref, att_ref):
    tq = q_ref.shape[0]
    blk = ATTN_BLOCK
    gwid = ATTN_GROUP_W
    inv_d = 1.0 / ATTN_HEAD_DIM

    qf = q_ref[...].astype(F32)
    q_sq = (qf * qf).astype(BF16)
    q_ss = jnp.concatenate([_dot(q_sq[:, g * gwid:(g + 1) * gwid], ek_ref[...])
                            for g in range(ATTN_KV_HEADS)], axis=1)
    qn = (qf * lax.rsqrt(q_ss * inv_d + EPS) * qg_ref[...] * (ATTN_HEAD_DIM ** -0.5)).astype(BF16)

    kf = jnp.concatenate([kp_ref[...], kc_ref[...]], axis=0).astype(F32)
    k_ss = _dot((kf * kf).astype(BF16), ek_ref[...])
    kn = (kf * lax.rsqrt(k_ss * inv_d + EPS) * kg_ref[...]).astype(BF16)
    k_rep = _dot(kn, rep_ref[...]).astype(BF16)
    v_all = jnp.concatenate([vp_ref[...], vc_ref[...]], axis=0)
    v_rep = _dot(v_all, rep_ref[...]).astype(BF16)

    lane = lax.broadcasted_iota(jnp.int32, (1, gwid), 1) // ATTN_HEAD_DIM
    key_col = lax.broadcasted_iota(jnp.int32, (blk, 2 * blk), 1)
    first_tile = pl.program_id(1) == 0

    cur_side = (lax.broadcasted_iota(jnp.int32, (blk, blk), 1)
                <= lax.broadcasted_iota(jnp.int32, (blk, blk), 0))

    def block(n):
        r0 = n * blk
        folded = []
        for g in range(ATTN_KV_HEADS):
            lanes_g = slice(g * gwid, (g + 1) * gwid)
            q_g = qn[r0:r0 + blk, lanes_g]
            k_g = k_rep[r0:r0 + 2 * blk, lanes_g]
            for hh in range(ATTN_REP):
                h = g * ATTN_REP + hh
                s = _dot_nt(jnp.where(lane == hh, q_g, jnp.zeros_like(q_g)), k_g) + bias_ref[h]
                if n == 0:
                    s = jnp.where(jnp.logical_and(first_tile, key_col < blk), NEG_INF, s)
                folded.append(jnp.maximum(s[:, :blk], s[:, blk:]))
        probs = []
        for h in range(ATTN_HEADS):
            sink = sink_ref[h]
            m = jnp.maximum(jnp.max(folded[h], axis=-1, keepdims=True), sink)
            pe = jnp.exp(folded[h] - m)
            den = jnp.sum(pe, axis=-1, keepdims=True) + jnp.exp(sink - m)
            pr = pe * (1.0 / den)
            probs.append(jnp.concatenate([jnp.where(cur_side, 0.0, pr), jnp.where(cur_side, pr, 0.0)],
                                         axis=1).astype(BF16))
        for g in range(ATTN_KV_HEADS):
            lanes_g = slice(g * gwid, (g + 1) * gwid)
            v_g = v_rep[r0:r0 + 2 * blk, lanes_g]
            v_blocks = [jnp.where(lane == hh, v_g, jnp.zeros_like(v_g)) for hh in range(ATTN_REP)]
            att_ref[r0:r0 + blk, lanes_g] = _dot(
                jnp.concatenate(probs[g * ATTN_REP:(g + 1) * ATTN_REP], axis=1),
                jnp.concatenate(v_blocks, axis=0))

    def finish():
        return _dot(att_ref[...].astype(BF16), w_ref[...])

    units = [(lambda n=n: block(n)) for n in range(tq // blk)]
    return units, finish


def _mixers_kernel(sink_ref, p_ref, halo_ref, kvp_ref, dt_ref, gb_ref,
                   dww_ref, dwb_ref, lng_ref, lnb_ref, wa_ref,
                   cw_ref, cb_ref, dtb_ref, a_ref, dexp_ref, ng_ref, wb_ref, e_ref, tril_ref,
                   bias_ref, qg_ref, kg_ref, ek_ref, rep_ref, wc_ref,
                   o_ref,
                   cbuf_ref, cacc_ref, xb_ref, xs_ref, bm_ref, cm_ref, y_ref, st_ref, att_ref):
    seg = lambda off, width: p_ref.at[:, off:off + width]
    gates_ref = seg(P_GATE, N_BRANCH * D_MODEL)
    cur_ref = seg(P_CONV, 2 * CONV_DIM)
    z_ref = seg(P_Z, SSD_INNER)
    xbc_ref = seg(P_XBC, SSD_XBC)
    q_ref = seg(P_Q, ATTN_DIM)
    kc_ref = seg(P_K, ATTN_KV_DIM)
    vc_ref = seg(P_V, ATTN_KV_DIM)
    kp_ref = kvp_ref.at[:, 0:ATTN_KV_DIM]
    vp_ref = kvp_ref.at[:, ATTN_KV_DIM:2 * ATTN_KV_DIM]
    conv_units, conv_finish = _conv_part(cur_ref, halo_ref, dww_ref, dwb_ref, lng_ref, lnb_ref, wa_ref,
                                         cbuf_ref, cacc_ref)
    ssd_units, ssd_finish = _ssd_part(z_ref, xbc_ref, dt_ref, cw_ref, cb_ref, dtb_ref, a_ref, dexp_ref,
                                      ng_ref, wb_ref, e_ref, tril_ref, xb_ref, xs_ref, bm_ref, cm_ref,
                                      y_ref, st_ref)
    swa_units, swa_finish = _swa_part(sink_ref, q_ref, kc_ref, vc_ref, kp_ref, vp_ref, bias_ref, qg_ref,
                                      kg_ref, ek_ref, rep_ref, wc_ref, att_ref)
    for unit in _round_robin(ssd_units, swa_units):
        unit()
    y_c = _gate(gates_ref, gb_ref, 2) * swa_finish()
    y_b = _gate(gates_ref, gb_ref, 1) * ssd_finish()
    for unit in conv_units:
        unit()
    y_a = _gate(gates_ref, gb_ref, 0) * conv_finish()
    o_ref[...] = (y_a + y_b + y_c).astype(BF16)


def _mixers(p, dt, sinks, gb, conv_consts, ssd_consts, swa_consts, batch, seq, tl):
    nl = seq // tl
    cw = 2 * CONV_DIM
    kvw = 2 * ATTN_KV_DIM
    assert P_V == P_K + ATTN_KV_DIM and P_K % kvw == 0 and P_CONV % cw == 0

    def prev_rows(rows, width, col_off):
        per_tile = tl // rows
        return pl.BlockSpec(
            (rows, width), lambda b, l: (jnp.maximum((b * nl + l) * per_tile - 1, 0), col_off // width))

    consts = (gb,) + tuple(conv_consts) + tuple(ssd_consts) + tuple(swa_consts)
    in_specs = (
        [pl.BlockSpec(memory_space=pltpu.SMEM),
         pl.BlockSpec((tl, P_COLS), lambda b, l: (b * nl + l, 0)),
         prev_rows(CONV_HALO, cw, P_CONV),
         prev_rows(ATTN_BLOCK, kvw, P_K),
         pl.BlockSpec((tl, DT_PAD), lambda b, l: (b * nl + l, 0))]
        + [_const_spec(a.shape) for a in consts])
    return pl.pallas_call(
        _mixers_kernel,
        grid=(batch, nl),
        in_specs=in_specs,
        out_specs=pl.BlockSpec((tl, D_MODEL), lambda b, l: (b * nl + l, 0)),
        out_shape=jax.ShapeDtypeStruct((batch * seq, D_MODEL), BF16),
        scratch_shapes=[
            pltpu.VMEM((CONV_HALO + tl, CONV_DIM), F32),
            pltpu.VMEM((tl, CONV_DIM), F32),
            pltpu.VMEM((SSD_TAIL + tl, SSD_XBC), F32),
            pltpu.VMEM((tl, SSD_INNER), F32),
            pltpu.VMEM((tl, SSD_BC), F32),
            pltpu.VMEM((tl, SSD_BC), F32),
            pltpu.VMEM((tl, SSD_INNER), F32),
            pltpu.VMEM((SSD_STATE, SSD_INNER), F32),
            pltpu.VMEM((tl, ATTN_DIM), F32),
        ],
        compiler_params=_cparams(("parallel", "arbitrary")),
        name="mixers",
    )(sinks, p, p, p, dt, *consts)


def _band_buckets():
    qi = np.arange(ATTN_BLOCK)[:, None] + ATTN_BLOCK
    kj = np.arange(2 * ATTN_BLOCK)[None, :]
    dist = qi - kj
    max_exact = REL_BUCKETS // 2
    d = np.maximum(dist, 1).astype(np.float32)
    large = max_exact + (np.log(d / np.float32(max_exact)) / np.float32(math.log(REL_MAX_DIST / max_exact))
                         * np.float32(REL_BUCKETS - max_exact)).astype(np.int32)
    large = np.minimum(large, REL_BUCKETS - 1)
    bucket = np.where(dist < max_exact, np.maximum(dist, 0), large).astype(np.int32)
    in_window = (dist >= 0) & (dist < ATTN_WINDOW)
    return bucket, in_window.astype(np.int32)


def _bias_kernel(tab_ref, bucket_ref, win_ref, o_ref):
    bucket = bucket_ref[...]
    win = win_ref[...] > 0
    for h in range(ATTN_HEADS):
        acc = jnp.zeros(bucket.shape, F32)
        for b in range(REL_BUCKETS):
            acc = jnp.where(bucket == b, tab_ref[b, h], acc)
        o_ref[h] = jnp.where(win, acc, NEG_INF)


def _band_bias(rel_table):
    bucket, win = _band_buckets()
    shape = (ATTN_BLOCK, 2 * ATTN_BLOCK)
    return pl.pallas_call(
        _bias_kernel,
        in_specs=[pl.BlockSpec(memory_space=pltpu.SMEM),
                  pl.BlockSpec(shape, lambda: (0, 0)),
                  pl.BlockSpec(shape, lambda: (0, 0))],
        out_specs=pl.BlockSpec((ATTN_HEADS,) + shape, lambda: (0, 0, 0)),
        out_shape=jax.ShapeDtypeStruct((ATTN_HEADS,) + shape, F32),
        name="band_bias",
    )(rel_table, jnp.asarray(bucket), jnp.asarray(win))


def _memkv_kernel(mem_ref, g_ref, w_ref, kg_ref, k_ref, v_ref):
    mn = _rms(mem_ref[0], g_ref[0]).astype(BF16)
    kv = _dot(mn, w_ref[0])
    parts = []
    for h in range(XATTN_HEADS):
        kh = kv[:, h * XATTN_HEAD_DIM:(h + 1) * XATTN_HEAD_DIM]
        parts.append(_rms(kh, kg_ref[0]))
    k_ref[0, 0] = jnp.concatenate(parts, axis=1).astype(BF16)
    v_ref[0, 0] = kv[:, D_MODEL:].astype(BF16)


def _mem_kv(mem, g, w, kg):
    depth = w.shape[0]
    batch = mem.shape[0]
    out = jax.ShapeDtypeStruct((depth, batch, MEM_LEN, D_MODEL), BF16)
    return pl.pallas_call(
        _memkv_kernel,
        grid=(depth, batch),
        in_specs=[
            pl.BlockSpec((1, MEM_LEN, D_MODEL), lambda i, b: (b, 0, 0)),
            pl.BlockSpec((1, 1, D_MODEL), lambda i, b: (i, 0, 0)),
            pl.BlockSpec((1, D_MODEL, 2 * D_MODEL), lambda i, b: (i, 0, 0)),
            pl.BlockSpec((1, 1, XATTN_HEAD_DIM), lambda i, b: (i, 0, 0)),
        ],
        out_specs=[pl.BlockSpec((1, 1, MEM_LEN, D_MODEL), lambda i, b: (i, b, 0, 0)),
                   pl.BlockSpec((1, 1, MEM_LEN, D_MODEL), lambda i, b: (i, b, 0, 0))],
        out_shape=[out, out],
        compiler_params=_cparams(("arbitrary", "arbitrary")),
        name="mem_kv",
    )(mem, g, w, kg)


MLP_CHUNK = D_MODEL


def _tail_body(h_ref, m_ref, k_ref, v_ref, wmix_ref, gx_ref, wq_ref, qg_ref, wo_ref, gm_ref,
               wup_ref, wdn_ref, att_ref):
    h = h_ref[...] + _dot(m_ref[...], wmix_ref[...])

    qx = _dot(_rms(h, gx_ref[...]).astype(BF16), wq_ref[...])
    lanes = [slice(hd * XATTN_HEAD_DIM, (hd + 1) * XATTN_HEAD_DIM) for hd in range(XATTN_HEADS)]
    scores = [_dot_nt((_rms(qx[:, ln], qg_ref[...]) * (XATTN_HEAD_DIM ** -0.5)).astype(BF16),
                      k_ref[0, 0, :, ln]) for ln in lanes]
    probs = []
    for s in scores:
        pe = jnp.exp(s - jnp.max(s, axis=-1, keepdims=True))
        probs.append((pe * (1.0 / jnp.sum(pe, axis=-1, keepdims=True))).astype(BF16))
    for ln, pr in zip(lanes, probs):
        att_ref[:, ln] = _dot(pr, v_ref[0, 0, :, ln])
    h = h + _dot(att_ref[...].astype(BF16), wo_ref[...])

    u = _rms(h, gm_ref[...]).astype(BF16)
    acc = h
    for cb in range(0, MLP_HIDDEN, MLP_CHUNK):
        up = jnp.maximum(_dot(u, wup_ref[:, cb:cb + MLP_CHUNK]), 0.0)
        acc = acc + _dot((up * up).astype(BF16), wdn_ref[cb:cb + MLP_CHUNK, :])
    return acc


def _tail_kernel(h_ref, m_ref, k_ref, v_ref, wmix_ref, gx_ref, wq_ref, qg_ref, wo_ref, gm_ref,
                 wup_ref, wdn_ref, gn_ref, wdt_ref, o_ref, u_ref, dt_ref, att_ref):
    h = _tail_body(h_ref, m_ref, k_ref, v_ref, wmix_ref, gx_ref, wq_ref, qg_ref, wo_ref, gm_ref,
                   wup_ref, wdn_ref, att_ref)
    o_ref[...] = h
    u = _rms(h, gn_ref[...]).astype(BF16)
    u_ref[...] = u
    dt_ref[...] = _dot(u, wdt_ref[...])


def _tail_last_kernel(h_ref, m_ref, k_ref, v_ref, wmix_ref, gx_ref, wq_ref, qg_ref, wo_ref, gm_ref,
                      wup_ref, wdn_ref, o_ref, att_ref):
    o_ref[...] = _tail_body(h_ref, m_ref, k_ref, v_ref, wmix_ref, gx_ref, wq_ref, qg_ref, wo_ref,
                            gm_ref, wup_ref, wdn_ref, att_ref)


def _tail(h2d, merged, k_mem, v_mem, layer, consts, next_consts, batch, seq, tl):
    nl = seq // tl
    t = batch * seq
    act = lambda b, l: (b * nl + l, 0)
    kv_spec = pl.BlockSpec((1, 1, MEM_LEN, D_MODEL), lambda b, l: (layer, b, 0, 0))
    in_specs = ([pl.BlockSpec((tl, D_MODEL), act), pl.BlockSpec((tl, D_MODEL), act), kv_spec, kv_spec]
                + [_const_spec(a.shape) for a in consts])
    out_specs = [pl.BlockSpec((tl, D_MODEL), act)]
    out_shape = [jax.ShapeDtypeStruct((t, D_MODEL), F32)]
    if next_consts is None:
        body, extra = _tail_last_kernel, ()
    else:
        body, extra = _tail_kernel, tuple(next_consts)
        in_specs += [_const_spec(a.shape) for a in extra]
        out_specs += [pl.BlockSpec((tl, D_MODEL), act), pl.BlockSpec((tl, DT_PAD), act)]
        out_shape += [jax.ShapeDtypeStruct((t, D_MODEL), BF16), jax.ShapeDtypeStruct((t, DT_PAD), F32)]
    return pl.pallas_call(
        body,
        grid=(batch, nl),
        in_specs=in_specs,
        out_specs=out_specs,
        out_shape=out_shape,
        scratch_shapes=[pltpu.VMEM((tl, D_MODEL), F32)],
        compiler_params=_cparams(("parallel", "arbitrary")),
        name="tail",
    )(h2d, merged, k_mem, v_mem, *consts, *extra)


def _head_indicator(width, head_dim):
    idx = np.arange(width) // head_dim
    return (idx[:, None] == idx[None, :]).astype(np.float32)


def _tile_sizes(seq):
    pick = lambda cands: next(c for c in cands if seq % c == 0)
    return dict(tm=pick((1024, 512, 256, 128)), tl_mix=pick((256, 128)), tl_tail=pick((512, 256, 128)))


def kernel(x, mem, rel_table, norm_mix, w_in, gate_bias, conv_dw_w, conv_dw_b, conv_ln_g, conv_ln_b, w_conv_out, ssd_conv_w, ssd_conv_b, ssd_dt_bias, ssd_A_log, ssd_D, ssd_norm_g, w_ssd_out, attn_q_norm, attn_k_norm, attn_sinks, w_attn_out, w_mix_out, norm_xattn, norm_mem, w_xq, w_xkv, xattn_q_norm, xattn_k_norm, w_xo, norm_mlp, w_mlp_up, w_mlp_down):
    batch, seq, _ = x.shape
    depth = w_in.shape[0]
    ts = _tile_sizes(seq)
    assert seq % ATTN_BLOCK == 0 and seq % SSD_CHUNK == 0

    w_p, w_dt = _regroup_weights(w_in)
    pad_h = ((0, 0), (0, DT_PAD - SSD_HEADS))
    dt_bias = jnp.pad(ssd_dt_bias, pad_h)[:, None, :]
    a_row = -jnp.exp(jnp.pad(ssd_A_log, pad_h))[:, None, :]
    d_exp = jnp.repeat(ssd_D, SSD_HEAD_DIM, axis=1)[:, None, :]

    e_np = np.zeros((DT_PAD, SSD_INNER), np.float32)
    e_np[np.arange(SSD_INNER) // SSD_HEAD_DIM, np.arange(SSD_INNER)] = 1.0
    e_mat = jnp.asarray(e_np, BF16)
    tril = jnp.asarray(np.tril(np.ones((SSD_CHUNK, SSD_CHUNK), np.float32)), BF16)
    ek = jnp.asarray(_head_indicator(ATTN_KV_DIM, ATTN_HEAD_DIM), BF16)
    rep_np = np.zeros((ATTN_KV_DIM, ATTN_DIM), np.float32)
    lanes = np.arange(ATTN_DIM)
    rep_np[(lanes // ATTN_GROUP_W) * ATTN_HEAD_DIM + lanes % ATTN_HEAD_DIM, lanes] = 1.0
    rep = jnp.asarray(rep_np, BF16)

    bias = _band_bias(rel_table)
    k_mem, v_mem = _mem_kv(mem, norm_mem[:, None, :], w_xkv.astype(BF16), xattn_k_norm[:, None, :])

    row = lambda a, i: a[i][None, :]
    h = x.reshape(batch * seq, D_MODEL)
    u, dt = _norm_dt(h, row(norm_mix, 0), w_dt[0], ts["tm"])
    for i in range(depth):
        p = _inproj(u, w_p, i, ts["tm"], P_COLS // 4)
        conv_consts = (conv_dw_w[i], row(conv_dw_b, i), row(conv_ln_g, i), row(conv_ln_b, i),
                       w_conv_out[i].astype(BF16))
        ssd_consts = (ssd_conv_w[i], row(ssd_conv_b, i), dt_bias[i], a_row[i], d_exp[i],
                      row(ssd_norm_g, i), w_ssd_out[i].astype(BF16), e_mat, tril)
        swa_consts = (bias, jnp.tile(attn_q_norm[i], ATTN_HEADS)[None, :],
                      jnp.tile(attn_k_norm[i], ATTN_KV_HEADS)[None, :], ek, rep,
                      w_attn_out[i].astype(BF16))
        merged = _mixers(p, dt, attn_sinks[i], gate_bias[i], conv_consts, ssd_consts, swa_consts,
                         batch, seq, ts["tl_mix"])
        tail_consts = (w_mix_out[i].astype(BF16), row(norm_xattn, i), w_xq[i].astype(BF16),
                       row(xattn_q_norm, i), w_xo[i].astype(BF16), row(norm_mlp, i),
                       w_mlp_up[i].astype(BF16), w_mlp_down[i].astype(BF16))
        if i + 1 < depth:
            h, u, dt = _tail(h, merged, k_mem, v_mem, i, tail_consts, (row(norm_mix, i + 1), w_dt[i + 1]),
                             batch, seq, ts["tl_tail"])
        else:
            (h,) = _tail(h, merged, k_mem, v_mem, i, tail_consts, None, batch, seq, ts["tl_tail"])
    return h.reshape(batch, seq, D_MODEL)
```

```python
import math

import numpy as np
import jax
import jax.numpy as jnp
from jax import lax
from jax.experimental import pallas as pl
from jax.experimental.pallas import tpu as pltpu

F32 = jnp.float32
BF16 = jnp.bfloat16

D_MODEL = 1024
MEM_LEN = 256
EPS = 1e-6
NEG_INF = -1e30

CONV_DIM = D_MODEL
CONV_KERNEL = 31

SSD_INNER = 2 * D_MODEL
SSD_HEAD_DIM = 64
SSD_HEADS = SSD_INNER // SSD_HEAD_DIM
SSD_GROUPS = 4
SSD_STATE = 128
SSD_CONV = 4
SSD_CHUNK = 128
SSD_BC = SSD_GROUPS * SSD_STATE
SSD_XBC = SSD_INNER + 2 * SSD_BC
SSD_GROUP_W = SSD_INNER // SSD_GROUPS

ATTN_HEADS = 16
ATTN_KV_HEADS = 4
ATTN_HEAD_DIM = 64
ATTN_DIM = ATTN_HEADS * ATTN_HEAD_DIM
ATTN_KV_DIM = ATTN_KV_HEADS * ATTN_HEAD_DIM
ATTN_REP = ATTN_HEADS // ATTN_KV_HEADS
ATTN_WINDOW = 128
ATTN_BLOCK = 128
ATTN_GROUP_W = ATTN_REP * ATTN_HEAD_DIM

REL_BUCKETS = 32
REL_MAX_DIST = 128

XATTN_HEADS = 4
XATTN_HEAD_DIM = D_MODEL // XATTN_HEADS

N_BRANCH = 3
MLP_HIDDEN = 4 * D_MODEL

OFF_CONV = 0
OFF_Z = OFF_CONV + 2 * CONV_DIM
OFF_XBC = OFF_Z + SSD_INNER
OFF_DT = OFF_XBC + SSD_XBC
OFF_Q = OFF_DT + SSD_HEADS
OFF_K = OFF_Q + ATTN_DIM
OFF_V = OFF_K + ATTN_KV_DIM
OFF_GATE = OFF_V + ATTN_KV_DIM
IN_COLS = OFF_GATE + N_BRANCH * D_MODEL

LANES = 128
SUBLANES = 8

P_GATE = 0
P_XBC = P_GATE + N_BRANCH * D_MODEL
P_CONV = P_XBC + SSD_XBC
P_Z = P_CONV + 2 * CONV_DIM
P_Q = P_Z + SSD_INNER
P_K = P_Q + ATTN_DIM
P_V = P_K + ATTN_KV_DIM
P_COLS = P_V + ATTN_KV_DIM
DT_PAD = LANES

VMEM_LIMIT = 56 * 1024 * 1024


def _cparams(semantics):
    return pltpu.CompilerParams(dimension_semantics=semantics, vmem_limit_bytes=VMEM_LIMIT)


def _const_spec(shape):
    nd = len(shape)
    return pl.BlockSpec(shape, lambda *_: (0,) * nd, pipeline_mode=pl.Buffered(1))


NEG_LOG2_E = -1.4426950408889634


def _sigmoid(x):
    return 1.0 / (1.0 + jnp.exp2(x * NEG_LOG2_E))


def _silu(x):
    return x * _sigmoid(x)


def _rms(x, g):
    return x * lax.rsqrt(jnp.mean(x * x, axis=-1, keepdims=True) + EPS) * g


def _dot(a, b):
    return jnp.dot(a, b, preferred_element_type=F32)


def _dot_nt(a, b):
    return lax.dot_general(a, b, (((1,), (1,)), ((), ())), preferred_element_type=F32)


def _shifted_taps(buf_ref, w_ref, offsets, r, rows, cols):
    acc = None
    for s in range(SUBLANES):
        group = [(j, o) for j, o in enumerate(offsets) if o % SUBLANES == s]
        if not group:
            continue
        n = rows if s == 0 else rows + SUBLANES
        t_s = None
        for j, o in group:
            base = r + o - s
            term = w_ref[j:j + 1, cols] * buf_ref[base:base + n, cols]
            t_s = term if t_s is None else t_s + term
        part = t_s if s == 0 else _shift_rows(t_s, s, rows)
        acc = part if acc is None else acc + part
    return acc


def _shift_rows(t, s, rows):
    tiles = [t[i:i + SUBLANES] for i in range(0, rows + SUBLANES, SUBLANES)]
    upper = lax.broadcasted_iota(jnp.int32, tiles[0].shape, 0) >= s
    return jnp.concatenate([pltpu.roll(jnp.where(upper, a, b), SUBLANES - s, axis=0)
                            for a, b in zip(tiles[:-1], tiles[1:])], axis=0)


def _split3(x):
    hi = x.astype(BF16)
    r1 = x - hi.astype(F32)
    mid = r1.astype(BF16)
    lo = (r1 - mid.astype(F32)).astype(BF16)
    return hi, mid, lo


def _norm_dt_kernel(x_ref, g_ref, wdt_ref, u_ref, dt_ref):
    u = _rms(x_ref[...], g_ref[...]).astype(BF16)
    u_ref[...] = u
    dt_ref[...] = _dot(u, wdt_ref[...])


def _norm_dt(h2d, g, wdt, tm):
    t = h2d.shape[0]
    return pl.pallas_call(
        _norm_dt_kernel,
        grid=(t // tm,),
        in_specs=[pl.BlockSpec((tm, D_MODEL), lambda i: (i, 0)),
                  _const_spec((1, D_MODEL)),
                  _const_spec((D_MODEL, DT_PAD))],
        out_specs=[pl.BlockSpec((tm, D_MODEL), lambda i: (i, 0)),
                   pl.BlockSpec((tm, DT_PAD), lambda i: (i, 0))],
        out_shape=[jax.ShapeDtypeStruct((t, D_MODEL), BF16),
                   jax.ShapeDtypeStruct((t, DT_PAD), F32)],
        compiler_params=_cparams(("parallel",)),
        name="norm_dt",
    )(h2d, g, wdt)


W_SEGMENTS = ((P_GATE, OFF_GATE, N_BRANCH * D_MODEL), (P_XBC, OFF_XBC, SSD_XBC),
              (P_CONV, OFF_CONV, 2 * CONV_DIM), (P_Z, OFF_Z, SSD_INNER),
              (P_Q, OFF_Q, ATTN_DIM + 2 * ATTN_KV_DIM))
REGROUP_ROWS = 128


def _lane_window(w_ref, start, width):
    lo = start // LANES * LANES
    hi = -(-(start + width) // LANES) * LANES
    return w_ref[:, lo:hi][:, start - lo:start - lo + width]


def _regroup_kernel(w_ref, o_ref, dt_ref):
    for dst, src, width in W_SEGMENTS:
        for c in range(0, width, LANES * 4):
            n = min(LANES * 4, width - c)
            o_ref[0, :, dst + c:dst + c + n] = _lane_window(w_ref, src + c, n).astype(BF16)
    dt = _lane_window(w_ref, OFF_DT, SSD_HEADS)
    dt_ref[0] = jnp.concatenate(
        [dt, jnp.zeros((dt.shape[0], DT_PAD - SSD_HEADS), F32)], axis=1).astype(BF16)


def _regroup_weights(w_in):
    depth, rows, cols = w_in.shape
    per_layer = rows // REGROUP_ROWS
    return pl.pallas_call(
        _regroup_kernel,
        grid=(depth, per_layer),
        in_specs=[pl.BlockSpec((REGROUP_ROWS, cols), lambda i, r: (i * per_layer + r, 0))],
        out_specs=[pl.BlockSpec((1, REGROUP_ROWS, P_COLS), lambda i, r: (i, r, 0)),
                   pl.BlockSpec((1, REGROUP_ROWS, DT_PAD), lambda i, r: (i, r, 0))],
        out_shape=[jax.ShapeDtypeStruct((depth, rows, P_COLS), BF16),
                   jax.ShapeDtypeStruct((depth, rows, DT_PAD), BF16)],
        compiler_params=_cparams(("parallel", "parallel")),
        name="regroup_weights",
    )(w_in.reshape(depth * rows, cols))


def _inproj_kernel(u_ref, w_ref, p_ref):
    p_ref[...] = _dot(u_ref[...], w_ref[...]).astype(BF16)


def _inproj(u, w_all, layer, tm, tn):
    t = u.shape[0]
    return pl.pallas_call(
        _inproj_kernel,
        grid=(t // tm, P_COLS // tn),
        in_specs=[pl.BlockSpec((tm, D_MODEL), lambda i, j: (i, 0)),
                  pl.BlockSpec((None, D_MODEL, tn), lambda i, j: (layer, 0, j))],
        out_specs=pl.BlockSpec((tm, tn), lambda i, j: (i, j)),
        out_shape=jax.ShapeDtypeStruct((t, P_COLS), BF16),
        compiler_params=_cparams(("parallel", "arbitrary")),
        name="inproj",
    )(u, w_all)


CONV_HALO = 32
CONV_ROWS = 128
SSD_TAIL = SUBLANES
SSD_CONV_ROWS = 64
SSD_CONV_COLS = 256


def _round_robin(*streams):
    total = max(len(s) for s in streams)
    order = []
    for step in range(total):
        for s in streams:
            lo = step * len(s) // total
            hi = (step + 1) * len(s) // total
            order.extend(s[lo:hi])
    return order


def _gate(gates_ref, gb_ref, branch):
    cols = slice(branch * D_MODEL, (branch + 1) * D_MODEL)
    return _sigmoid(gates_ref[:, cols].astype(F32) + gb_ref[branch:branch + 1, :])


def _conv_part(cur_ref, halo_ref, dww_ref, dwb_ref, lng_ref, lnb_ref, w_ref, buf_ref, acc_ref):
    tl = cur_ref.shape[0]
    c = CONV_DIM
    cur = cur_ref[...].astype(F32)
    buf_ref[CONV_HALO:CONV_HALO + tl, :] = cur[:, :c] * _sigmoid(cur[:, c:])
    hal = halo_ref[...].astype(F32)
    hglu = hal[:, :c] * _sigmoid(hal[:, c:])
    buf_ref[0:CONV_HALO, :] = jnp.where(pl.program_id(1) > 0, hglu, 0.0)

    first = CONV_HALO - (CONV_KERNEL - 1)
    offsets = [first + j for j in range(CONV_KERNEL)]

    def taps(r, cb):
        cols = slice(cb, cb + LANES)
        acc_ref[r:r + CONV_ROWS, cols] = (
            _shifted_taps(buf_ref, dww_ref, offsets, r, CONV_ROWS, cols) + dwb_ref[:, cols])

    def finish():
        y = acc_ref[...]
        mu = jnp.mean(y, axis=-1, keepdims=True)
        yc = y - mu
        yn = yc * lax.rsqrt(jnp.mean(yc * yc, axis=-1, keepdims=True) + EPS)
        yn = _silu(yn * lng_ref[...] + lnb_ref[...])
        return _dot(yn.astype(BF16), w_ref[...])

    units = [(lambda r=r, cb=cb: taps(r, cb)) for r in range(0, tl, CONV_ROWS) for cb in range(0, c, LANES)]
    return units, finish


def _ssd_part(z_ref, xbc_ref, dt_ref, cw_ref, cb_ref, dtb_ref, a_ref, dexp_ref, ng_ref, w_ref,
              e_ref, tril_ref, xb_ref, xs_ref, bm_ref, cm_ref, y_ref, st_ref):
    tl = z_ref.shape[0]
    q = SSD_CHUNK
    gw = SSD_GROUP_W

    @pl.when(pl.program_id(1) == 0)
    def _():
        st_ref[...] = jnp.zeros_like(st_ref)
        xb_ref[0:SSD_TAIL, :] = jnp.zeros((SSD_TAIL, SSD_XBC), F32)

    xb_ref[SSD_TAIL:SSD_TAIL + tl, :] = xbc_ref[...].astype(F32)
    first = SSD_TAIL - (SSD_CONV - 1)
    offsets = [first + j for j in range(SSD_CONV)]
    for r in range(0, tl, SSD_CONV_ROWS):
        rows = slice(r, r + SSD_CONV_ROWS)
        for cbk in range(0, SSD_XBC, SSD_CONV_COLS):
            cols = slice(cbk, cbk + SSD_CONV_COLS)
            act = _silu(_shifted_taps(xb_ref, cw_ref, offsets, r, SSD_CONV_ROWS, cols) + cb_ref[:, cols])
            if cbk < SSD_INNER:
                xs_ref[rows, cols] = act
            elif cbk < SSD_INNER + SSD_BC:
                bm_ref[rows, cbk - SSD_INNER:cbk - SSD_INNER + SSD_CONV_COLS] = act
            else:
                off = cbk - SSD_INNER - SSD_BC
                cm_ref[rows, off:off + SSD_CONV_COLS] = act
    xb_ref[0:SSD_TAIL, :] = xb_ref[tl:tl + SSD_TAIL, :]

    x = dt_ref[...] + dtb_ref[...]
    dt_all = jnp.maximum(x, 0.0) + jnp.log1p(jnp.exp(-jnp.abs(x)))

    row = lax.broadcasted_iota(jnp.int32, (q, q), 0)
    col = lax.broadcasted_iota(jnp.int32, (q, q), 1)
    tri = row >= col
    lane = lax.broadcasted_iota(jnp.int32, (q, LANES), 1)
    lo_half = lane < SSD_HEAD_DIM
    tril = tril_ref[...]

    ctx = {}

    def chunk_head(r0):
        rows = slice(r0, r0 + q)
        dt_c = dt_all[rows]
        da = dt_c * a_ref[...]
        hi, mid, lo = _split3(da)
        cs = _dot(tril, hi) + _dot(tril, mid) + _dot(tril, lo)
        adj_t = (cs - jnp.log(dt_c)).T
        cs_last = cs[q - 1:q, :]
        w_dec = jnp.exp(cs_last - cs) * dt_c
        dfs = jnp.exp(cs)
        cd = jnp.exp(cs_last)
        cd_hi = cd.astype(BF16)
        cd_lo = (cd - cd_hi.astype(F32)).astype(BF16)
        stack = jnp.concatenate([
            w_dec.astype(BF16), dfs.astype(BF16),
            jnp.broadcast_to(cd_hi, (SUBLANES * 2, LANES)),
            jnp.broadcast_to(cd_lo, (SUBLANES * 2, LANES))], axis=0)
        ex = _dot(stack, e_ref[...])
        w_exp = ex[0:q]
        dfs_exp = ex[q:2 * q]
        cd_exp = ex[2 * q:2 * q + 1] + ex[2 * q + 2 * SUBLANES:2 * q + 2 * SUBLANES + 1]

        ctx[r0] = (cs, adj_t, w_exp, dfs_exp, cd_exp)

    def chunk_body(r0):
        rows = slice(r0, r0 + q)
        cs, adj_t, w_exp, dfs_exp, cd_exp = ctx[r0]
        groups = range(SSD_GROUPS)
        pairs = range(gw // LANES)
        lanes = [slice(g * gw, (g + 1) * gw) for g in groups]
        xs_g = [xs_ref[rows, lanes[g]] for g in groups]
        b_g = [bm_ref[rows, g * SSD_STATE:(g + 1) * SSD_STATE] for g in groups]
        c_g = [cm_ref[rows, g * SSD_STATE:(g + 1) * SSD_STATE].astype(BF16) for g in groups]
        st_g = [st_ref[:, lanes[g]] for g in groups]
        cbm = [_dot_nt(c_g[g], b_g[g].astype(BF16)) for g in groups]
        y_off = [_dot(c_g[g], st_g[g].astype(BF16)) * dfs_exp[:, lanes[g]] for g in groups]
        lhs = {}
        for g in groups:
            for pr in pairs:
                h0 = (g * gw + pr * LANES) // SSD_HEAD_DIM
                sc = []
                for h in (h0, h0 + 1):
                    dec = jnp.where(tri, jnp.exp(cs[:, h:h + 1] - adj_t[h:h + 1, :]), 0.0)
                    sc.append((cbm[g] * dec).astype(BF16))
                lhs[g, pr] = jnp.concatenate(sc, axis=1)
        for g in groups:
            for pr in pairs:
                lanes_p = slice(pr * LANES, (pr + 1) * LANES)
                xp = xs_g[g][:, lanes_p].astype(BF16)
                zero = jnp.zeros_like(xp)
                rhs = jnp.concatenate([jnp.where(lo_half, xp, zero),
                                       jnp.where(lo_half, zero, xp)], axis=0)
                y_ref[rows, g * gw + pr * LANES:g * gw + (pr + 1) * LANES] = (
                    y_off[g][:, lanes_p] + _dot(lhs[g, pr], rhs))
        for g in groups:
            xw = (xs_g[g] * w_exp[:, lanes[g]]).astype(BF16)
            st_ref[:, lanes[g]] = st_g[g] * cd_exp[:, lanes[g]] + _dot(b_g[g].T.astype(BF16), xw)

    def finish():
        y = y_ref[...] + xs_ref[...] * dexp_ref[...]
        y = y * _silu(z_ref[...].astype(F32))
        parts = []
        for g in range(SSD_GROUPS):
            yg = y[:, g * gw:(g + 1) * gw]
            parts.append(yg * lax.rsqrt(jnp.mean(yg * yg, axis=-1, keepdims=True) + EPS))
        yn = jnp.concatenate(parts, axis=1) * ng_ref[...]
        return _dot(yn.astype(BF16), w_ref[...])

    for r0 in range(0, tl, q):
        chunk_head(r0)
    units = [(lambda r0=r0: chunk_body(r0)) for r0 in range(0, tl, q)]
    return units, finish


def _swa_part(sink_ref, q_ref, kc_ref, vc_ref, kp_ref, vp_ref, bias_ref, qg_ref, kg_ref, ek_ref,
              rep_ref, w_ref, att_ref):
    tq = q_ref.shape[0]
    blk = ATTN_BLOCK
    gwid = ATTN_GROUP_W
    inv_d = 1.0 / ATTN_HEAD_DIM

    qf = q_ref[...].astype(F32)
    q_sq = (qf * qf).astype(BF16)
    q_ss = jnp.concatenate([_dot(q_sq[:, g * gwid:(g + 1) * gwid], ek_ref[...])
                            for g in range(ATTN_KV_HEADS)], axis=1)
    qn = (qf * lax.rsqrt(q_ss * inv_d + EPS) * qg_ref[...] * (ATTN_HEAD_DIM ** -0.5)).astype(BF16)

    kf = jnp.concatenate([kp_ref[...], kc_ref[...]], axis=0).astype(F32)
    k_ss = _dot((kf * kf).astype(BF16), ek_ref[...])
    kn = (kf * lax.rsqrt(k_ss * inv_d + EPS) * kg_ref[...]).astype(BF16)
    k_rep = _dot(kn, rep_ref[...]).astype(BF16)
    v_all = jnp.concatenate([vp_ref[...], vc_ref[...]], axis=0)
    v_rep = _dot(v_all, rep_ref[...]).astype(BF16)

    lane = lax.broadcasted_iota(jnp.int32, (1, gwid), 1) // ATTN_HEAD_DIM
    key_col = lax.broadcasted_iota(jnp.int32, (blk, 2 * blk), 1)
    first_tile = pl.program_id(1) == 0

    cur_side = (lax.broadcasted_iota(jnp.int32, (blk, blk), 1)
                <= lax.broadcasted_iota(jnp.int32, (blk, blk), 0))

    def block(n):
        r0 = n * blk
        folded = []
        for g in range(ATTN_KV_HEADS):
            lanes_g = slice(g * gwid, (g + 1) * gwid)
            q_g = qn[r0:r0 + blk, lanes_g]
            k_g = k_rep[r0:r0 + 2 * blk, lanes_g]
            for hh in range(ATTN_REP):
                h = g * ATTN_REP + hh
                s = _dot_nt(jnp.where(lane == hh, q_g, jnp.zeros_like(q_g)), k_g) + bias_ref[h]
                if n == 0:
                    s = jnp.where(jnp.logical_and(first_tile, key_col < blk), NEG_INF, s)
                folded.append(jnp.maximum(s[:, :blk], s[:, blk:]))
        probs = []
        for h in range(ATTN_HEADS):
            sink = sink_ref[h]
            m = jnp.maximum(jnp.max(folded[h], axis=-1, keepdims=True), sink)
            pe = jnp.exp(folded[h] - m)
            den = jnp.sum(pe, axis=-1, keepdims=True) + jnp.exp(sink - m)
            pr = pe * (1.0 / den)
            probs.append(jnp.concatenate([jnp.where(cur_side, 0.0, pr), jnp.where(cur_side, pr, 0.0)],
                                         axis=1).astype(BF16))
        for g in range(ATTN_KV_HEADS):
            lanes_g = slice(g * gwid, (g + 1) * gwid)
            v_g = v_rep[r0:r0 + 2 * blk, lanes_g]
            v_blocks = [jnp.where(lane == hh, v_g, jnp.zeros_like(v_g)) for hh in range(ATTN_REP)]
            att_ref[r0:r0 + blk, lanes_g] = _dot(
                jnp.concatenate(probs[g * ATTN_REP:(g + 1) * ATTN_REP], axis=1),
                jnp.concatenate(v_blocks, axis=0))

    def finish():
        return _dot(att_ref[...].astype(BF16), w_ref[...])

    units = [(lambda n=n: block(n)) for n in range(tq // blk)]
    return units, finish


def _mixers_kernel(sink_ref, p_ref, halo_ref, kvp_ref, dt_ref, gb_ref,
                   dww_ref, dwb_ref, lng_ref, lnb_ref, wa_ref,
                   cw_ref, cb_ref, dtb_ref, a_ref, dexp_ref, ng_ref, wb_ref, e_ref, tril_ref,
                   bias_ref, qg_ref, kg_ref, ek_ref, rep_ref, wc_ref,
                   o_ref,
                   cbuf_ref, cacc_ref, xb_ref, xs_ref, bm_ref, cm_ref, y_ref, st_ref, att_ref):
    seg = lambda off, width: p_ref.at[:, off:off + width]
    gates_ref = seg(P_GATE, N_BRANCH * D_MODEL)
    cur_ref = seg(P_CONV, 2 * CONV_DIM)
    z_ref = seg(P_Z, SSD_INNER)
    xbc_ref = seg(P_XBC, SSD_XBC)
    q_ref = seg(P_Q, ATTN_DIM)
    kc_ref = seg(P_K, ATTN_KV_DIM)
    vc_ref = seg(P_V, ATTN_KV_DIM)
    kp_ref = kvp_ref.at[:, 0:ATTN_KV_DIM]
    vp_ref = kvp_ref.at[:, ATTN_KV_DIM:2 * ATTN_KV_DIM]
    conv_units, conv_finish = _conv_part(cur_ref, halo_ref, dww_ref, dwb_ref, lng_ref, lnb_ref, wa_ref,
                                         cbuf_ref, cacc_ref)
    ssd_units, ssd_finish = _ssd_part(z_ref, xbc_ref, dt_ref, cw_ref, cb_ref, dtb_ref, a_ref, dexp_ref,
                                      ng_ref, wb_ref, e_ref, tril_ref, xb_ref, xs_ref, bm_ref, cm_ref,
                                      y_ref, st_ref)
    swa_units, swa_finish = _swa_part(sink_ref, q_ref, kc_ref, vc_ref, kp_ref, vp_ref, bias_ref, qg_ref,
                                      kg_ref, ek_ref, rep_ref, wc_ref, att_ref)
    for unit in _round_robin(ssd_units, swa_units):
        unit()
    y_c = _gate(gates_ref, gb_ref, 2) * swa_finish()
    y_b = _gate(gates_ref, gb_ref, 1) * ssd_finish()
    for unit in conv_units:
        unit()
    y_a = _gate(gates_ref, gb_ref, 0) * conv_finish()
    o_ref[...] = (y_a + y_b + y_c).astype(BF16)


def _mixers(p, dt, sinks, gb, conv_consts, ssd_consts, swa_consts, batch, seq, tl):
    nl = seq // tl
    cw = 2 * CONV_DIM
    kvw = 2 * ATTN_KV_DIM
    assert P_V == P_K + ATTN_KV_DIM and P_K % kvw == 0 and P_CONV % cw == 0

    def prev_rows(rows, width, col_off):
        per_tile = tl // rows
        return pl.BlockSpec(
            (rows, width), lambda b, l: (jnp.maximum((b * nl + l) * per_tile - 1, 0), col_off // width))

    consts = (gb,) + tuple(conv_consts) + tuple(ssd_consts) + tuple(swa_consts)
    in_specs = (
        [pl.BlockSpec(memory_space=pltpu.SMEM),
         pl.BlockSpec((tl, P_COLS), lambda b, l: (b * nl + l, 0)),
         prev_rows(CONV_HALO, cw, P_CONV),
         prev_rows(ATTN_BLOCK, kvw, P_K),
         pl.BlockSpec((tl, DT_PAD), lambda b, l: (b * nl + l, 0))]
        + [_const_spec(a.shape) for a in consts])
    return pl.pallas_call(
        _mixers_kernel,
        grid=(batch, nl),
        in_specs=in_specs,
        out_specs=pl.BlockSpec((tl, D_MODEL), lambda b, l: (b * nl + l, 0)),
        out_shape=jax.ShapeDtypeStruct((batch * seq, D_MODEL), BF16),
        scratch_shapes=[
            pltpu.VMEM((CONV_HALO + tl, CONV_DIM), F32),
            pltpu.VMEM((tl, CONV_DIM), F32),
            pltpu.VMEM((SSD_TAIL + tl, SSD_XBC), F32),
            pltpu.VMEM((tl, SSD_INNER), F32),
            pltpu.VMEM((tl, SSD_BC), F32),
            pltpu.VMEM((tl, SSD_BC), F32),
            pltpu.VMEM((tl, SSD_INNER), F32),
            pltpu.VMEM((SSD_STATE, SSD_INNER), F32),
            pltpu.VMEM((tl, ATTN_DIM), F32),
        ],
        compiler_params=_cparams(("parallel", "arbitrary")),
        name="mixers",
    )(sinks, p, p, p, dt, *consts)


def _band_buckets():
    qi = np.arange(ATTN_BLOCK)[:, None] + ATTN_BLOCK
    kj = np.arange(2 * ATTN_BLOCK)[None, :]
    dist = qi - kj
    max_exact = REL_BUCKETS // 2
    d = np.maximum(dist, 1).astype(np.float32)
    large = max_exact + (np.log(d / np.float32(max_exact)) / np.float32(math.log(REL_MAX_DIST / max_exact))
                         * np.float32(REL_BUCKETS - max_exact)).astype(np.int32)
    large = np.minimum(large, REL_BUCKETS - 1)
    bucket = np.where(dist < max_exact, np.maximum(dist, 0), large).astype(np.int32)
    in_window = (dist >= 0) & (dist < ATTN_WINDOW)
    return bucket, in_window.astype(np.int32)


def _bias_kernel(tab_ref, bucket_ref, win_ref, o_ref):
    bucket = bucket_ref[...]
    win = win_ref[...] > 0
    for h in range(ATTN_HEADS):
        acc = jnp.zeros(bucket.shape, F32)
        for b in range(REL_BUCKETS):
            acc = jnp.where(bucket == b, tab_ref[b, h], acc)
        o_ref[h] = jnp.where(win, acc, NEG_INF)


def _band_bias(rel_table):
    bucket, win = _band_buckets()
    shape = (ATTN_BLOCK, 2 * ATTN_BLOCK)
    return pl.pallas_call(
        _bias_kernel,
        in_specs=[pl.BlockSpec(memory_space=pltpu.SMEM),
                  pl.BlockSpec(shape, lambda: (0, 0)),
                  pl.BlockSpec(shape, lambda: (0, 0))],
        out_specs=pl.BlockSpec((ATTN_HEADS,) + shape, lambda: (0, 0, 0)),
        out_shape=jax.ShapeDtypeStruct((ATTN_HEADS,) + shape, F32),
        name="band_bias",
    )(rel_table, jnp.asarray(bucket), jnp.asarray(win))


def _memkv_kernel(mem_ref, g_ref, w_ref, kg_ref, k_ref, v_ref):
    mn = _rms(mem_ref[0], g_ref[0]).astype(BF16)
    kv = _dot(mn, w_ref[0])
    parts = []
    for h in range(XATTN_HEADS):
        kh = kv[:, h * XATTN_HEAD_DIM:(h + 1) * XATTN_HEAD_DIM]
        parts.append(_rms(kh, kg_ref[0]))
    k_ref[0, 0] = jnp.concatenate(parts, axis=1).astype(BF16)
    v_ref[0, 0] = kv[:, D_MODEL:].astype(BF16)


def _mem_kv(mem, g, w, kg):
    depth = w.shape[0]
    batch = mem.shape[0]
    out = jax.ShapeDtypeStruct((depth, batch, MEM_LEN, D_MODEL), BF16)
    return pl.pallas_call(
        _memkv_kernel,
        grid=(depth, batch),
        in_specs=[
            pl.BlockSpec((1, MEM_LEN, D_MODEL), lambda i, b: (b, 0, 0)),
            pl.BlockSpec((1, 1, D_MODEL), lambda i, b: (i, 0, 0)),
            pl.BlockSpec((1, D_MODEL, 2 * D_MODEL), lambda i, b: (i, 0, 0)),
            pl.BlockSpec((1, 1, XATTN_HEAD_DIM), lambda i, b: (i, 0, 0)),
        ],
        out_specs=[pl.BlockSpec((1, 1, MEM_LEN, D_MODEL), lambda i, b: (i, b, 0, 0)),
                   pl.BlockSpec((1, 1, MEM_LEN, D_MODEL), lambda i, b: (i, b, 0, 0))],
        out_shape=[out, out],
        compiler_params=_cparams(("arbitrary", "arbitrary")),
        name="mem_kv",
    )(mem, g, w, kg)


MLP_CHUNK = D_MODEL


def _tail_body(h_ref, m_ref, k_ref, v_ref, wmix_ref, gx_ref, wq_ref, qg_ref, wo_ref, gm_ref,
               wup_ref, wdn_ref, att_ref):
    h = h_ref[...] + _dot(m_ref[...], wmix_ref[...])

    qx = _dot(_rms(h, gx_ref[...]).astype(BF16), wq_ref[...])
    lanes = [slice(hd * XATTN_HEAD_DIM, (hd + 1) * XATTN_HEAD_DIM) for hd in range(XATTN_HEADS)]
    scores = [_dot_nt((_rms(qx[:, ln], qg_ref[...]) * (XATTN_HEAD_DIM ** -0.5)).astype(BF16),
                      k_ref[0, 0, :, ln]) for ln in lanes]
    probs = []
    for s in scores:
        pe = jnp.exp(s - jnp.max(s, axis=-1, keepdims=True))
        probs.append((pe * (1.0 / jnp.sum(pe, axis=-1, keepdims=True))).astype(BF16))
    for ln, pr in zip(lanes, probs):
        att_ref[:, ln] = _dot(pr, v_ref[0, 0, :, ln])
    h = h + _dot(att_ref[...].astype(BF16), wo_ref[...])

    u = _rms(h, gm_ref[...]).astype(BF16)
    acc = h
    for cb in range(0, MLP_HIDDEN, MLP_CHUNK):
        up = jnp.maximum(_dot(u, wup_ref[:, cb:cb + MLP_CHUNK]), 0.0)
        acc = acc + _dot((up * up).astype(BF16), wdn_ref[cb:cb + MLP_CHUNK, :])
    return acc


def _tail_kernel(h_ref, m_ref, k_ref, v_ref, wmix_ref, gx_ref, wq_ref, qg_ref, wo_ref, gm_ref,
                 wup_ref, wdn_ref, gn_ref, wdt_ref, o_ref, u_ref, dt_ref, att_ref):
    h = _tail_body(h_ref, m_ref, k_ref, v_ref, wmix_ref, gx_ref, wq_ref, qg_ref, wo_ref, gm_ref,
                   wup_ref, wdn_ref, att_ref)
    o_ref[...] = h
    u = _rms(h, gn_ref[...]).astype(BF16)
    u_ref[...] = u
    dt_ref[...] = _dot(u, wdt_ref[...])


def _tail_last_kernel(h_ref, m_ref, k_ref, v_ref, wmix_ref, gx_ref, wq_ref, qg_ref, wo_ref, gm_ref,
                      wup_ref, wdn_ref, o_ref, att_ref):
    o_ref[...] = _tail_body(h_ref, m_ref, k_ref, v_ref, wmix_ref, gx_ref, wq_ref, qg_ref, wo_ref,
                            gm_ref, wup_ref, wdn_ref, att_ref)


def _tail(h2d, merged, k_mem, v_mem, layer, consts, next_consts, batch, seq, tl):
    nl = seq // tl
    t = batch * seq
    act = lambda b, l: (b * nl + l, 0)
    kv_spec = pl.BlockSpec((1, 1, MEM_LEN, D_MODEL), lambda b, l: (layer, b, 0, 0))
    in_specs = ([pl.BlockSpec((tl, D_MODEL), act), pl.BlockSpec((tl, D_MODEL), act), kv_spec, kv_spec]
                + [_const_spec(a.shape) for a in consts])
    out_specs = [pl.BlockSpec((tl, D_MODEL), act)]
    out_shape = [jax.ShapeDtypeStruct((t, D_MODEL), F32)]
    if next_consts is None:
        body, extra = _tail_last_kernel, ()
    else:
        body, extra = _tail_kernel, tuple(next_consts)
        in_specs += [_const_spec(a.shape) for a in extra]
        out_specs += [pl.BlockSpec((tl, D_MODEL), act), pl.BlockSpec((tl, DT_PAD), act)]
        out_shape += [jax.ShapeDtypeStruct((t, D_MODEL), BF16), jax.ShapeDtypeStruct((t, DT_PAD), F32)]
    return pl.pallas_call(
        body,
        grid=(batch, nl),
        in_specs=in_specs,
        out_specs=out_specs,
        out_shape=out_shape,
        scratch_shapes=[pltpu.VMEM((tl, D_MODEL), F32)],
        compiler_params=_cparams(("parallel", "arbitrary")),
        name="tail",
    )(h2d, merged, k_mem, v_mem, *consts, *extra)


def _head_indicator(width, head_dim):
    idx = np.arange(width) // head_dim
    return (idx[:, None] == idx[None, :]).astype(np.float32)


def _tile_sizes(seq):
    pick = lambda cands: next(c for c in cands if seq % c == 0)
    return dict(tm=pick((1024, 512, 256, 128)), tl_mix=pick((256, 128)), tl_tail=pick((512, 256, 128)))


def kernel(x, mem, rel_table, norm_mix, w_in, gate_bias, conv_dw_w, conv_dw_b, conv_ln_g, conv_ln_b, w_conv_out, ssd_conv_w, ssd_conv_b, ssd_dt_bias, ssd_A_log, ssd_D, ssd_norm_g, w_ssd_out, attn_q_norm, attn_k_norm, attn_sinks, w_attn_out, w_mix_out, norm_xattn, norm_mem, w_xq, w_xkv, xattn_q_norm, xattn_k_norm, w_xo, norm_mlp, w_mlp_up, w_mlp_down):
    batch, seq, _ = x.shape
    depth = w_in.shape[0]
    ts = _tile_sizes(seq)
    assert seq % ATTN_BLOCK == 0 and seq % SSD_CHUNK == 0

    w_p, w_dt = _regroup_weights(w_in)
    pad_h = ((0, 0), (0, DT_PAD - SSD_HEADS))
    dt_bias = jnp.pad(ssd_dt_bias, pad_h)[:, None, :]
    a_row = -jnp.exp(jnp.pad(ssd_A_log, pad_h))[:, None, :]
    d_exp = jnp.repeat(ssd_D, SSD_HEAD_DIM, axis=1)[:, None, :]

    e_np = np.zeros((DT_PAD, SSD_INNER), np.float32)
    e_np[np.arange(SSD_INNER) // SSD_HEAD_DIM, np.arange(SSD_INNER)] = 1.0
    e_mat = jnp.asarray(e_np, BF16)
    tril = jnp.asarray(np.tril(np.ones((SSD_CHUNK, SSD_CHUNK), np.float32)), BF16)
    ek = jnp.asarray(_head_indicator(ATTN_KV_DIM, ATTN_HEAD_DIM), BF16)
    rep_np = np.zeros((ATTN_KV_DIM, ATTN_DIM), np.float32)
    lanes = np.arange(ATTN_DIM)
    rep_np[(lanes // ATTN_GROUP_W) * ATTN_HEAD_DIM + lanes % ATTN_HEAD_DIM, lanes] = 1.0
    rep = jnp.asarray(rep_np, BF16)

    bias = _band_bias(rel_table)
    k_mem, v_mem = _mem_kv(mem, norm_mem[:, None, :], w_xkv.astype(BF16), xattn_k_norm[:, None, :])

    row = lambda a, i: a[i][None, :]
    h = x.reshape(batch * seq, D_MODEL)
    u, dt = _norm_dt(h, row(norm_mix, 0), w_dt[0], ts["tm"])
    for i in range(depth):
        p = _inproj(u, w_p, i, ts["tm"], P_COLS // 4)
        conv_consts = (conv_dw_w[i], row(conv_dw_b, i), row(conv_ln_g, i), row(conv_ln_b, i),
                       w_conv_out[i].astype(BF16))
        ssd_consts = (ssd_conv_w[i], row(ssd_conv_b, i), dt_bias[i], a_row[i], d_exp[i],
                      row(ssd_norm_g, i), w_ssd_out[i].astype(BF16), e_mat, tril)
        swa_consts = (bias, jnp.tile(attn_q_norm[i], ATTN_HEADS)[None, :],
                      jnp.tile(attn_k_norm[i], ATTN_KV_HEADS)[None, :], ek, rep,
                      w_attn_out[i].astype(BF16))
        merged = _mixers(p, dt, attn_sinks[i], gate_bias[i], conv_consts, ssd_consts, swa_consts,
                         batch, seq, ts["tl_mix"])
        tail_consts = (w_mix_out[i].astype(BF16), row(norm_xattn, i), w_xq[i].astype(BF16),
                       row(xattn_q_norm, i), w_xo[i].astype(BF16), row(norm_mlp, i),
                       w_mlp_up[i].astype(BF16), w_mlp_down[i].astype(BF16))
        if i + 1 < depth:
            h, u, dt = _tail(h, merged, k_mem, v_mem, i, tail_consts, (row(norm_mix, i + 1), w_dt[i + 1]),
                             batch, seq, ts["tl_tail"])
        else:
            (h,) = _tail(h, merged, k_mem, v_mem, i, tail_consts, None, batch, seq, ts["tl_tail"])
    return h.reshape(batch, seq, D_MODEL)
```
